```python
import math
import jax, jax.numpy as jnp
from jax import lax
import numpy as np

D_MODEL = 2048
BATCH = 2
SEQ = 8192
DEPTH = 4
DEC_BATCH = 8
DEC_SEQ = 64
PAST_LEN = 2048

CHUNK = 64
N_PAST_CHUNKS = 8
BAND_PAST = CHUNK * N_PAST_CHUNKS
BAND = BAND_PAST + CHUNK
N_HEADS = 16
HEAD_DIM = D_MODEL // N_HEADS
MAX_REL = 128
N_REL = 2 * MAX_REL + 1
SSM_GROUP = 16
N_GROUPS = D_MODEL // SSM_GROUP
SSM_STATE = 64
D_FF = 4 * D_MODEL
N_MIXERS = 2
N_ATTN = (DEPTH + 1) // 2
N_SSM = DEPTH // 2
RMS_EPS = 1e-5
NEG_INF = -1e30
DT_MIN = 0.001
DT_MAX = 0.1

kernel_name = "chunk_band_attn_s5_hybrid_stream_step"


def rms_norm(x, g):
    x32 = x.astype(jnp.float32)
    y = x32 * lax.rsqrt(jnp.mean(x32 * x32, axis=-1, keepdims=True) + RMS_EPS)
    return (y * g.astype(jnp.float32)).astype(x.dtype)


def sq_relu_mlp(x, w_up, w_down):
    h = jax.nn.relu(x @ w_up)
    return ((h * h) @ w_down).astype(x.dtype)


def band_attend(q, k, v, q_pos, k_pos, rel_bias):
    s = jnp.einsum('bqhd,bkhd->bhqk', q, k).astype(jnp.float32) * (HEAD_DIM ** -0.5)
    rel = jnp.clip(q_pos[:, None] - k_pos[None, :], -MAX_REL, MAX_REL) + MAX_REL
    s = s + rel_bias.astype(jnp.float32)[:, rel][None]
    q_chunk = q_pos // CHUNK
    k_chunk = k_pos // CHUNK
    mask = ((k_pos[None, :] >= 0)
            & (k_chunk[None, :] <= q_chunk[:, None])
            & (k_chunk[None, :] >= q_chunk[:, None] - N_PAST_CHUNKS))
    s = jnp.where(mask[None, None], s, NEG_INF)
    p = jax.nn.softmax(s, axis=-1).astype(v.dtype)
    return jnp.einsum('bhqk,bkhd->bqhd', p, v)


def attn_qkv(h, w_qkv):
    b, t, _ = h.shape
    q, k, v = jnp.split(h @ w_qkv, 3, axis=-1)
    shp = (b, t, N_HEADS, HEAD_DIM)
    return q.reshape(shp), k.reshape(shp), v.reshape(shp)


def attn_prompt(h, w_qkv, w_o, rel_bias):
    b, s, _ = h.shape
    q, k, v = attn_qkv(h, w_qkv)
    nc = s // CHUNK
    zpad = jnp.zeros((b, BAND_PAST, N_HEADS, HEAD_DIM), k.dtype)
    k_pad = jnp.concatenate([zpad, k], axis=1)
    v_pad = jnp.concatenate([zpad, v], axis=1)
    q_blocks = q.reshape(b, nc, CHUNK, N_HEADS, HEAD_DIM).swapaxes(0, 1)

    def one_chunk(args):
        c, q_blk = args
        start = c * CHUNK
        k_band = lax.dynamic_slice_in_dim(k_pad, start, BAND, axis=1)
        v_band = lax.dynamic_slice_in_dim(v_pad, start, BAND, axis=1)
        q_pos = start + jnp.arange(CHUNK, dtype=jnp.int32)
        k_pos = start - BAND_PAST + jnp.arange(BAND, dtype=jnp.int32)
        return band_attend(q_blk, k_band, v_band, q_pos, k_pos, rel_bias)

    o = lax.map(one_chunk, (jnp.arange(nc, dtype=jnp.int32), q_blocks))
    o = o.swapaxes(0, 1).reshape(b, s, D_MODEL)
    keep = min(BAND_PAST, s)
    return (o @ w_o).astype(h.dtype), k[:, s - keep:], v[:, s - keep:]


def attn_sample(h, k_cache, v_cache, w_qkv, w_o, rel_bias):
    b, t, _ = h.shape
    q, k, v = attn_qkv(h, w_qkv)
    l_c = k_cache.shape[1]
    k_band = jnp.concatenate([k_cache.astype(k.dtype), k], axis=1)
    v_band = jnp.concatenate([v_cache.astype(v.dtype), v], axis=1)
    q_pos = PAST_LEN + jnp.arange(t, dtype=jnp.int32)
    k_pos = PAST_LEN - l_c + jnp.arange(l_c + t, dtype=jnp.int32)
    o = band_attend(q, k_band, v_band, q_pos, k_pos, rel_bias).reshape(b, t, D_MODEL)
    return (o @ w_o).astype(h.dtype), k, v


def s5_discretise(a_re, a_im, log_dt, b_re, b_im):
    a_re = a_re.astype(jnp.float32)
    a_im = a_im.astype(jnp.float32)
    b_re = b_re.astype(jnp.float32)
    b_im = b_im.astype(jnp.float32)
    dt = jnp.exp(log_dt.astype(jnp.float32))[:, None]
    mag = jnp.exp(dt * a_re)
    abar_re = mag * jnp.cos(dt * a_im)
    abar_im = mag * jnp.sin(dt * a_im)
    den = a_re * a_re + a_im * a_im
    num_re = abar_re - 1.0
    num_im = abar_im
    z_re = ((num_re * a_re + num_im * a_im) / den)[..., None]
    z_im = ((num_im * a_re - num_re * a_im) / den)[..., None]
    bbar_re = z_re * b_re - z_im * b_im
    bbar_im = z_re * b_im + z_im * b_re
    return abar_re, abar_im, bbar_re, bbar_im


def _ssm_combine(e1, e2):
    a1r, a1i, b1r, b1i = e1
    a2r, a2i, b2r, b2i = e2
    ar = a1r * a2r - a1i * a2i
    ai = a1r * a2i + a1i * a2r
    br = a2r * b1r - a2i * b1i + b2r
    bi = a2r * b1i + a2i * b1r + b2i
    return ar, ai, br, bi


def s5_block(u, x0_re, x0_im, abar_re, abar_im, bbar_re, bbar_im, c_re, c_im, d):
    b, t, _ = u.shape
    u32 = u.astype(jnp.float32)
    ug = u32.reshape(b, t, N_GROUPS, SSM_GROUP)
    bu_re = jnp.einsum('btgs,gps->btgp', ug, bbar_re)
    bu_im = jnp.einsum('btgs,gps->btgp', ug, bbar_im)
    a_r = jnp.broadcast_to(abar_re, bu_re.shape)
    a_i = jnp.broadcast_to(abar_im, bu_re.shape)
    acc_re, acc_im, h_re, h_im = lax.associative_scan(_ssm_combine, (a_r, a_i, bu_re, bu_im), axis=1)
    x0r = x0_re[:, None]
    x0i = x0_im[:, None]
    h_re = h_re + acc_re * x0r - acc_im * x0i
    h_im = h_im + acc_re * x0i + acc_im * x0r
    y = (jnp.einsum('btgp,gsp->btgs', h_re, c_re.astype(jnp.float32))
         - jnp.einsum('btgp,gsp->btgs', h_im, c_im.astype(jnp.float32)))
    y = y.reshape(b, t, D_MODEL) + d.astype(jnp.float32) * u32
    return y, h_re[:, -1], h_im[:, -1]


def s5_glu(y, w_a, w_b, dtype):
    g = jax.nn.gelu(y).astype(dtype)
    return ((g @ w_a) * jax.nn.sigmoid(g @ w_b)).astype(dtype)


def ssm_prompt(h, disc, c_re, c_im, d, w_a, w_b):
    b, s, _ = h.shape
    nc = s // CHUNK
    u_blocks = h.reshape(b, nc, CHUNK, D_MODEL).swapaxes(0, 1)
    x0 = jnp.zeros((b, N_GROUPS, SSM_STATE), jnp.float32)

    def step(carry, u_blk):
        xr, xi = carry
        y, xr, xi = s5_block(u_blk, xr, xi, *disc, c_re, c_im, d)
        return (xr, xi), y

    (xr, xi), ys = lax.scan(step, (x0, x0), u_blocks)
    y = ys.swapaxes(0, 1).reshape(b, s, D_MODEL)
    return s5_glu(y, w_a, w_b, h.dtype), xr, xi


def ssm_sample(h, st_re, st_im, disc, c_re, c_im, d, w_a, w_b):
    y, xr, xi = s5_block(h, st_re.astype(jnp.float32), st_im.astype(jnp.float32), *disc, c_re, c_im, d)
    return s5_glu(y, w_a, w_b, h.dtype), xr, xi


def setup_inputs(seed: int = 0) -> dict:
    key = jax.random.key(seed)
    ks = jax.random.split(key, 26)
    f32 = jnp.float32
    cache_len = min(BAND_PAST, PAST_LEN)
    nrm = lambda k, shp, sc: jax.random.normal(k, shp, f32) * sc
    a_im_base = (math.pi * jnp.arange(SSM_STATE, dtype=f32))[None, None, :]
    return {
        "x_prompt": nrm(ks[0], (BATCH, SEQ, D_MODEL), 1.0),
        "x_sample": nrm(ks[1], (DEC_BATCH, DEC_SEQ, D_MODEL), 1.0),
        "cache_attn_k": nrm(ks[2], (N_ATTN, DEC_BATCH, cache_len, N_HEADS, HEAD_DIM), 1.0),
        "cache_attn_v": nrm(ks[3], (N_ATTN, DEC_BATCH, cache_len, N_HEADS, HEAD_DIM), 1.0),
        "state_ssm_re": nrm(ks[4], (N_SSM, DEC_BATCH, N_GROUPS, SSM_STATE), 0.5),
        "state_ssm_im": nrm(ks[5], (N_SSM, DEC_BATCH, N_GROUPS, SSM_STATE), 0.5),
        "norm_mix": 1.0 + nrm(ks[6], (DEPTH, D_MODEL), 0.02),
        "norm_mlp": 1.0 + nrm(ks[7], (DEPTH, D_MODEL), 0.02),
        "norm_final": 1.0 + nrm(ks[8], (D_MODEL,), 0.02),
        "attn_w_qkv": nrm(ks[9], (N_ATTN, D_MODEL, 3 * D_MODEL), D_MODEL ** -0.5),
        "attn_w_o": nrm(ks[10], (N_ATTN, D_MODEL, D_MODEL), D_MODEL ** -0.5),
        "attn_rel_bias": nrm(ks[11], (N_ATTN, N_HEADS, N_REL), 0.1),
        "ssm_a_re": -0.5 + nrm(ks[12], (N_SSM, N_GROUPS, SSM_STATE), 0.01),
        "ssm_a_im": a_im_base + nrm(ks[13], (N_SSM, N_GROUPS, SSM_STATE), 0.01),
        "ssm_log_dt": jax.random.uniform(ks[14], (N_SSM, N_GROUPS), f32, math.log(DT_MIN), math.log(DT_MAX)),
        "ssm_b_re": nrm(ks[15], (N_SSM, N_GROUPS, SSM_STATE, SSM_GROUP), (2 * SSM_GROUP) ** -0.5),
        "ssm_b_im": nrm(ks[16], (N_SSM, N_GROUPS, SSM_STATE, SSM_GROUP), (2 * SSM_GROUP) ** -0.5),
        "ssm_c_re": nrm(ks[17], (N_SSM, N_GROUPS, SSM_GROUP, SSM_STATE), SSM_STATE ** -0.5),
        "ssm_c_im": nrm(ks[18], (N_SSM, N_GROUPS, SSM_GROUP, SSM_STATE), SSM_STATE ** -0.5),
        "ssm_d": nrm(ks[19], (N_SSM, D_MODEL), 1.0),
        "ssm_w_glu_a": nrm(ks[20], (N_SSM, D_MODEL, D_MODEL), D_MODEL ** -0.5),
        "ssm_w_glu_b": nrm(ks[21], (N_SSM, D_MODEL, D_MODEL), D_MODEL ** -0.5),
        "mlp_w_up": nrm(ks[22], (DEPTH, D_MODEL, D_FF), D_MODEL ** -0.5),
        "mlp_w_down": nrm(ks[23], (DEPTH, D_FF, D_MODEL), D_FF ** -0.5),
    }


def reference(x_prompt, x_sample, cache_attn_k, cache_attn_v, state_ssm_re, state_ssm_im,
              norm_mix, norm_mlp, norm_final, attn_w_qkv, attn_w_o, attn_rel_bias,
              ssm_a_re, ssm_a_im, ssm_log_dt, ssm_b_re, ssm_b_im, ssm_c_re, ssm_c_im, ssm_d,
              ssm_w_glu_a, ssm_w_glu_b, mlp_w_up, mlp_w_down):
    xp = x_prompt
    xs = x_sample
    k_p, v_p, k_s, v_s = [], [], [], []
    sr_p, si_p, sr_s, si_s = [], [], [], []
    for i in range(DEPTH):
        j = i // N_MIXERS
        hp = rms_norm(xp, norm_mix[i])
        hs = rms_norm(xs, norm_mix[i])
        if i % N_MIXERS == 0:
            op, kp, vp = attn_prompt(hp, attn_w_qkv[j], attn_w_o[j], attn_rel_bias[j])
            os_, kn, vn = attn_sample(hs, cache_attn_k[j], cache_attn_v[j],
                                      attn_w_qkv[j], attn_w_o[j], attn_rel_bias[j])
            k_p.append(kp); v_p.append(vp); k_s.append(kn); v_s.append(vn)
        else:
            disc = s5_discretise(ssm_a_re[j], ssm_a_im[j], ssm_log_dt[j], ssm_b_re[j], ssm_b_im[j])
            op, rp, ip = ssm_prompt(hp, disc, ssm_c_re[j], ssm_c_im[j], ssm_d[j],
                                    ssm_w_glu_a[j], ssm_w_glu_b[j])
            os_, rn, inn = ssm_sample(hs, state_ssm_re[j], state_ssm_im[j], disc, ssm_c_re[j],
                                      ssm_c_im[j], ssm_d[j], ssm_w_glu_a[j], ssm_w_glu_b[j])
            sr_p.append(rp); si_p.append(ip); sr_s.append(rn); si_s.append(inn)
        xp = xp + op
        xs = xs + os_
        xp = xp + sq_relu_mlp(rms_norm(xp, norm_mlp[i]), mlp_w_up[i], mlp_w_down[i])
        xs = xs + sq_relu_mlp(rms_norm(xs, norm_mlp[i]), mlp_w_up[i], mlp_w_down[i])
    y_prompt = rms_norm(xp, norm_final)
    y_sample = rms_norm(xs, norm_final)
    new_k_prompt = jnp.stack(k_p)
    new_v_prompt = jnp.stack(v_p)
    new_ssm_re_prompt = jnp.stack(sr_p)
    new_ssm_im_prompt = jnp.stack(si_p)
    new_k_sample = jnp.stack(k_s)
    new_v_sample = jnp.stack(v_s)
    new_ssm_re_sample = jnp.stack(sr_s)
    new_ssm_im_sample = jnp.stack(si_s)
    return (y_prompt, y_sample, new_k_prompt, new_v_prompt, new_ssm_re_prompt, new_ssm_im_prompt,
            new_k_sample, new_v_sample, new_ssm_re_sample, new_ssm_im_sample)
```

```python
import functools
import math

import jax
import jax.numpy as jnp
from jax import lax
from jax.experimental import pallas as pl
from jax.experimental.pallas import tpu as pltpu

CHUNK = 64
N_PAST_CHUNKS = 8
RMS_EPS = 1e-5
NEG_INF = -1e30

LANES = 128
SUBLANES_BF16 = 16
VMEM_LIMIT_BYTES = 58 * 1024 * 1024

PAIR = 2 * CHUNK
BAND_PAIRS = N_PAST_CHUNKS // 2 + 1
BAND = BAND_PAIRS * PAIR
SSM_L = 16
KV_TAIL = CHUNK * N_PAST_CHUNKS

F32 = jnp.float32
BF16 = jnp.bfloat16


def _params(*sem):
    return pltpu.CompilerParams(dimension_semantics=sem, vmem_limit_bytes=VMEM_LIMIT_BYTES)


def _tile(n, target, mult=SUBLANES_BF16):
    best = None
    for t in range(mult, min(n, target) + 1, mult):
        if n % t == 0:
            best = t
    assert best is not None, (n, target, mult)
    return best


def _rms(x, g):
    y = x * lax.rsqrt(jnp.mean(x * x, axis=-1, keepdims=True) + RMS_EPS)
    return y * g


def _dot(a, b):
    return jnp.dot(a, b, preferred_element_type=F32)


def _dot_nt(a, b):
    return lax.dot_general(a, b, (((1,), (1,)), ((), ())), preferred_element_type=F32)


def _qkv_kernel(x_ref, g_ref, w_ref, o_ref, tail_ref):
    hn = _rms(x_ref[...], g_ref[...]).astype(BF16)
    acc = _dot(hn, w_ref[...])
    o_ref[...] = acc.astype(BF16)
    tail_ref[...] = acc


def _qkv(x, gain, w, n_prompt_tiles_per_batch, n_prompt_batches):
    t, d = x.shape
    n = w.shape[1]
    tm = KV_TAIL
    tn = d
    nt = t // tm
    n_prompt_tiles = n_prompt_tiles_per_batch * n_prompt_batches
    n_slots = n_prompt_batches + (nt - n_prompt_tiles)

    def slot(i):
        return jnp.where(i < n_prompt_tiles, i // n_prompt_tiles_per_batch,
                         i - n_prompt_tiles + n_prompt_batches)

    return pl.pallas_call(
        _qkv_kernel,
        grid=(n // tn, nt),
        in_specs=[pl.BlockSpec((tm, d), lambda j, i: (i, 0)),
                  pl.BlockSpec((1, d), lambda j, i: (0, 0)),
                  pl.BlockSpec((d, tn), lambda j, i: (0, j))],
        out_specs=[pl.BlockSpec((tm, tn), lambda j, i: (i, j)),
                   pl.BlockSpec((tm, tn), lambda j, i: (slot(i), j))],
        out_shape=[jax.ShapeDtypeStruct((t, n), BF16),
                   jax.ShapeDtypeStruct((n_slots * tm, n), F32)],
        compiler_params=_params("arbitrary", "arbitrary"),
        name="qkv",
    )(x, gain, w)


def _proj_res_kernel(x_ref, a_ref, w_ref, o_ref):
    o_ref[...] = x_ref[...] + _dot(a_ref[...], w_ref[...])


def _proj_res(x, a, w):
    t, d = x.shape
    k = a.shape[1]
    tm = _tile(t, 512)
    return pl.pallas_call(
        _proj_res_kernel,
        grid=(t // tm,),
        in_specs=[pl.BlockSpec((tm, d), lambda i: (i, 0)),
                  pl.BlockSpec((tm, k), lambda i: (i, 0)),
                  pl.BlockSpec((k, d), lambda i: (0, 0))],
        out_specs=pl.BlockSpec((tm, d), lambda i: (i, 0)),
        out_shape=jax.ShapeDtypeStruct((t, d), F32),
        compiler_params=_params("arbitrary"),
        name="proj_res",
    )(x, a, w)


def _mlp_kernel(x_ref, g_ref, wu_ref, wd_ref, o_ref, hn_ref):
    k = pl.program_id(1)

    @pl.when(k == 0)
    def _():
        hn_ref[...] = _rms(x_ref[...], g_ref[...]).astype(BF16)

    h = jnp.maximum(_dot(hn_ref[...], wu_ref[...]), 0.0)
    d = _dot((h * h).astype(BF16), wd_ref[...])

    @pl.when(k == 0)
    def _():
        o_ref[...] = x_ref[...] + d

    @pl.when(k > 0)
    def _():
        o_ref[...] += d


def _mlp(x, gain, w_up, w_down):
    t, d = x.shape
    dff = w_up.shape[1]
    tm = _tile(t, 768)
    tf = _tile(dff, 1024, LANES)
    return pl.pallas_call(
        _mlp_kernel,
        grid=(t // tm, dff // tf),
        in_specs=[pl.BlockSpec((tm, d), lambda i, k: (i, 0)),
                  pl.BlockSpec((1, d), lambda i, k: (0, 0)),
                  pl.BlockSpec((d, tf), lambda i, k: (0, k)),
                  pl.BlockSpec((tf, d), lambda i, k: (k, 0))],
        out_specs=pl.BlockSpec((tm, d), lambda i, k: (i, 0)),
        out_shape=jax.ShapeDtypeStruct((t, d), F32),
        scratch_shapes=[pltpu.VMEM((tm, d), BF16)],
        compiler_params=_params("arbitrary", "arbitrary"),
        name="mlp",
    )(x, gain, w_up, w_down)


def _final_norm_kernel(x_ref, g_ref, yp_ref, ys_ref, *, n_prompt_tiles):
    i = pl.program_id(0)
    y = _rms(x_ref[...], g_ref[...])

    @pl.when(i < n_prompt_tiles)
    def _():
        yp_ref[...] = y

    @pl.when(i >= n_prompt_tiles)
    def _():
        ys_ref[...] = y


def _final_norm(x, gain, n_prompt_rows):
    t, d = x.shape
    n_sample_rows = t - n_prompt_rows
    tm = _tile(math.gcd(n_prompt_rows, n_sample_rows), 512)
    npt = n_prompt_rows // tm
    return pl.pallas_call(
        functools.partial(_final_norm_kernel, n_prompt_tiles=npt),
        grid=(t // tm,),
        in_specs=[pl.BlockSpec((tm, d), lambda i: (i, 0)),
                  pl.BlockSpec((1, d), lambda i: (0, 0))],
        out_specs=[pl.BlockSpec((tm, d), lambda i: (jnp.minimum(i, npt - 1), 0)),
                   pl.BlockSpec((tm, d), lambda i: (jnp.maximum(i - npt, 0), 0))],
        out_shape=[jax.ShapeDtypeStruct((n_prompt_rows, d), F32),
                   jax.ShapeDtypeStruct((n_sample_rows, d), F32)],
        compiler_params=_params("arbitrary"),
        name="final_norm",
    )(x, gain)


def _bias_kernel(t_ref, o_ref):
    n_heads = o_ref.shape[0]
    width = t_ref.shape[1]
    i = lax.broadcasted_iota(jnp.int32, (PAIR, BAND), 0)
    j = lax.broadcasted_iota(jnp.int32, (PAIR, BAND), 1)
    kb = j // PAIR
    jj = j % PAIR
    masked = ((kb == BAND_PAIRS - 1) & (i < CHUNK) & (jj >= CHUNK)) | \
             ((kb == 0) & (i >= CHUNK) & (jj < CHUNK))
    for h in range(n_heads):
        rows = jnp.broadcast_to(t_ref[h:h + 1, :], (PAIR, width))
        rolled = pltpu.roll(rows, width - (PAIR - 1), 1, stride=1, stride_axis=0)
        o_ref[h] = jnp.where(masked, NEG_INF, rolled[:, :BAND])


def _bias_table(rel_bias):
    n_heads, n_rel = rel_bias.shape
    max_rel = (n_rel - 1) // 2
    assert 2 * max_rel == 2 * PAIR and BAND - PAIR == KV_TAIL
    width = BAND + PAIR
    tvec = jnp.concatenate(
        [jnp.broadcast_to(rel_bias[:, 2 * max_rel:], (n_heads, width - 2 * max_rel)),
         jnp.flip(rel_bias[:, :2 * max_rel], axis=1)], axis=1)
    return pl.pallas_call(
        _bias_kernel,
        out_shape=jax.ShapeDtypeStruct((n_heads, PAIR, BAND), F32),
        compiler_params=pltpu.CompilerParams(vmem_limit_bytes=VMEM_LIMIT_BYTES),
        name="attn_bias",
    )(tvec)


def _softmax_pv(scores, values):
    m = scores[0].max(axis=1, keepdims=True)
    for s in scores[1:]:
        m = jnp.maximum(m, s.max(axis=1, keepdims=True))
    acc = None
    l = None
    for s, v in zip(scores, values):
        p = jnp.exp(s - m)
        ls = p.sum(axis=1, keepdims=True)
        pv = _dot(p.astype(BF16), v)
        acc = pv if acc is None else acc + pv
        l = ls if l is None else l + ls
    return acc / l


def _attn_prompt_kernel(q_ref, k_ref, v_ref, b_ref, o_ref, *, n_heads, dh, n_pb):
    qi = pl.program_id(2)
    scale = dh ** -0.5

    def one(q_rows, h, key_rows, bias):
        hs = slice(h * dh, (h + 1) * dh)
        q = q_ref[q_rows, hs]
        s = _dot_nt(q, k_ref[key_rows, hs]) * scale + bias
        o = _softmax_pv([s], [v_ref[key_rows, hs]])
        o_ref[q_rows, hs] = o.astype(BF16)

    @pl.when(qi == 0)
    def _():
        for pb in range(n_pb):
            for h in range(n_heads):
                nk = (pb + 1) * PAIR
                one(slice(pb * PAIR, (pb + 1) * PAIR), h, slice(0, nk), b_ref[h, :, BAND - nk:])

    @pl.when(qi > 0)
    def _():
        def body(pb, carry):
            q0 = pl.multiple_of(pb * PAIR, PAIR)
            k0 = pl.multiple_of((qi * n_pb + pb - (BAND_PAIRS - 1)) * PAIR, PAIR)
            for h in range(n_heads):
                one(pl.ds(q0, PAIR), h, pl.ds(k0, BAND), b_ref[h])
            return carry
        lax.fori_loop(0, n_pb, body, 0)


def _attn_prompt(qkv, bias, n_batches, seq, d, dh, t_total):
    n_pb = BAND_PAIRS - 1
    tq = n_pb * PAIR
    assert seq % tq == 0 and seq >= BAND
    hw = min(d, 4 * dh)
    n_hg = d // hw
    n_heads = hw // dh
    return pl.pallas_call(
        functools.partial(_attn_prompt_kernel, n_heads=n_heads, dh=dh, n_pb=n_pb),
        grid=(n_batches, n_hg, seq // tq),
        in_specs=[pl.BlockSpec((tq, hw), lambda b, g, i: (b * (seq // tq) + i, g)),
                  pl.BlockSpec((seq, hw), lambda b, g, i: (b, n_hg + g)),
                  pl.BlockSpec((seq, hw), lambda b, g, i: (b, 2 * n_hg + g)),
                  pl.BlockSpec((n_heads, PAIR, BAND), lambda b, g, i: (g, 0, 0))],
        out_specs=pl.BlockSpec((tq, hw), lambda b, g, i: (b * (seq // tq) + i, g)),
        out_shape=jax.ShapeDtypeStruct((t_total, d), BF16),
        compiler_params=_params("arbitrary", "arbitrary", "arbitrary"),
        name="attn_prompt",
    )(qkv, qkv, qkv, bias)


def _attn_sample_kernel(q_ref, kn_ref, vn_ref, kc_ref, vc_ref, b_ref, oin_ref, o_ref, *, n_heads, dh):
    del oin_ref
    scale = dh ** -0.5
    ts = q_ref.shape[0]
    lc = kc_ref.shape[0]
    for h in range(n_heads):
        hs = slice(h * dh, (h + 1) * dh)
        q = q_ref[:, hs]
        s_c = _dot_nt(q, kc_ref[:, hs].astype(BF16)) * scale + b_ref[h, :ts, :lc]
        s_n = _dot_nt(q, kn_ref[:, hs]) * scale + b_ref[h, :ts, lc:lc + ts]
        o = _softmax_pv([s_c, s_n], [vc_ref[:, hs].astype(BF16), vn_ref[:, hs]])
        o_ref[:, hs] = o.astype(BF16)


def _attn_sample(qkv, k_cache, v_cache, bias, o_all, row0, d, dh):
    n_b, lc, _ = k_cache.shape
    ts = CHUNK
    assert lc == KV_TAIL and row0 % ts == 0
    n_heads = d // dh
    r0 = row0 // ts
    return pl.pallas_call(
        functools.partial(_attn_sample_kernel, n_heads=n_heads, dh=dh),
        grid=(n_b,),
        in_specs=[pl.BlockSpec((ts, d), lambda b: (r0 + b, 0)),
                  pl.BlockSpec((ts, d), lambda b: (r0 + b, 1)),
                  pl.BlockSpec((ts, d), lambda b: (r0 + b, 2)),
                  pl.BlockSpec((None, lc, d), lambda b: (b, 0, 0)),
                  pl.BlockSpec((None, lc, d), lambda b: (b, 0, 0)),
                  pl.BlockSpec((n_heads, PAIR, BAND), lambda b: (0, 0, 0)),
                  pl.BlockSpec(memory_space=pl.ANY)],
        out_specs=pl.BlockSpec((ts, d), lambda b: (r0 + b, 0)),
        out_shape=jax.ShapeDtypeStruct(o_all.shape, o_all.dtype),
        input_output_aliases={6: 0},
        compiler_params=_params("arbitrary"),
        name="attn_sample",
    )(qkv, qkv, qkv, k_cache, v_cache, bias, o_all)


def _norm_blocked_kernel(x_ref, g_ref, o_ref):
    hn = _rms(x_ref[...], g_ref[...]).astype(BF16)
    for gb in range(o_ref.shape[0]):
        o_ref[gb] = hn[:, gb * LANES:(gb + 1) * LANES]


def _norm_blocked(x, gain):
    t, d = x.shape
    nc = t // SSM_L
    tc = _tile(nc, 384)
    n_gb = d // LANES
    return pl.pallas_call(
        _norm_blocked_kernel,
        grid=(SSM_L, nc // tc),
        in_specs=[pl.BlockSpec((tc, d), lambda s, c: (c, s)),
                  pl.BlockSpec((1, d), lambda s, c: (0, 0))],
        out_specs=pl.BlockSpec((n_gb, tc, LANES), lambda s, c: (0, c, s)),
        out_shape=jax.ShapeDtypeStruct((n_gb, nc, SSM_L * LANES), BF16),
        compiler_params=_params("arbitrary", "arbitrary"),
        name="ssm_norm",
    )(x.reshape(nc, SSM_L * d), gain)


def _gelu_tanh(x):
    cdf = 0.5 * (1.0 + jnp.tanh(math.sqrt(2.0 / math.pi) * (x + 0.044715 * (x ** 3))))
    return x * cdf


def _split_bf16(x):
    hi = x.astype(BF16)
    lo = (x - hi.astype(F32)).astype(BF16)
    return hi, lo


def _ssm_core_kernel(u_ref, x0_ref, prow_ref, pcol_ref, bxr_ref, bxi_ref, cxr_ref, cxi_ref, d_ref,
                     g_ref, xfp_ref, xfs_ref,
                     w_scr, cmh_scr, cml_scr, toep_scr, s_scr, xst_scr,
                     *, n_prompt, blocks_per_prompt, n_sample, blocks_per_sample, row_tile,
                     groups_per_tile):
    n_state = bxr_ref.shape[1]
    ch = bxr_ref.shape[0]

    def discretise(a_re, a_im, log_dt):
        dt = jnp.exp(log_dt)
        mag = jnp.exp(dt * a_re)
        l_re = mag * jnp.cos(dt * a_im)
        l_im = mag * jnp.sin(dt * a_im)
        den = a_re * a_re + a_im * a_im
        n_re = l_re - 1.0
        n_im = l_im
        z_re = (n_re * a_re + n_im * a_im) / den
        z_im = (n_im * a_re - n_re * a_im) / den
        return l_re, l_im, z_re, z_im

    def powers(l_re, l_im, n):
        out = [(jnp.ones_like(l_re), jnp.zeros_like(l_im))]
        for _ in range(n):
            pr, pi = out[-1]
            out.append((pr * l_re - pi * l_im, pr * l_im + pi * l_re))
        return out

    lr, li, zr, zi = discretise(prow_ref[0:1, :], prow_ref[1:2, :], prow_ref[2:3, :])
    prow = powers(lr, li, SSM_L)
    lcr, lci, _, _ = discretise(pcol_ref[:, 0:1], pcol_ref[:, 1:2], pcol_ref[:, 2:3])
    pcol = powers(lcr, lci, SSM_L)

    p = n_state // groups_per_tile
    sg = ch // groups_per_tile
    mb = (lax.broadcasted_iota(jnp.int32, (ch, n_state), 0) // sg ==
          lax.broadcasted_iota(jnp.int32, (ch, n_state), 1) // p)
    mc = (lax.broadcasted_iota(jnp.int32, (n_state, ch), 0) // p ==
          lax.broadcasted_iota(jnp.int32, (n_state, ch), 1) // sg)

    bxr = bxr_ref[...]
    bxi = bxi_ref[...]
    bb_r = jnp.where(mb, zr * bxr - zi * bxi, 0.0)
    bb_i = jnp.where(mb, zr * bxi + zi * bxr, 0.0)

    for s in range(SSM_L):
        pr, pi = prow[SSM_L - 1 - s]
        rows = slice(s * ch, (s + 1) * ch)
        w_scr[rows, 0:n_state] = (pr * bb_r - pi * bb_i).astype(BF16)
        w_scr[rows, n_state:2 * n_state] = (pr * bb_i + pi * bb_r).astype(BF16)

    cxr = cxr_ref[...]
    cxi = cxi_ref[...]
    for tau in range(SSM_L + 1):
        pr, pi = pcol[tau]
        m_re = jnp.where(mc, cxr * pr - cxi * pi, 0.0)
        m_im = jnp.where(mc, cxr * pi + cxi * pr, 0.0)
        cols = slice(tau * ch, (tau + 1) * ch)
        hi_r, lo_r = _split_bf16(m_re)
        hi_i, lo_i = _split_bf16(-m_im)
        cmh_scr[0:n_state, cols] = hi_r
        cmh_scr[n_state:2 * n_state, cols] = hi_i
        if tau < SSM_L:
            cml_scr[0:n_state, cols] = lo_r
            cml_scr[n_state:2 * n_state, cols] = lo_i

    a_hi_r, a_lo_r = _split_bf16(bb_r)
    a_hi_i, a_lo_i = _split_bf16(bb_i)
    a_hi = jnp.concatenate([a_hi_r, a_hi_i], axis=1)
    a_lo = jnp.concatenate([a_lo_r, a_lo_i], axis=1)
    width = SSM_L * ch
    k_all = (_dot(a_hi, cmh_scr[:, 0:width]) + _dot(a_lo, cmh_scr[:, 0:width])
             + _dot(a_hi, cml_scr[...]))
    k_all = k_all.astype(BF16)
    for s in range(SSM_L):
        rows = slice(s * ch, (s + 1) * ch)
        if s > 0:
            toep_scr[rows, 0:s * ch] = jnp.zeros((ch, s * ch), BF16)
        toep_scr[rows, s * ch:width] = k_all[:, 0:width - s * ch]

    n_rows = u_ref.shape[0]
    for r in range(n_rows // row_tile):
        rows = slice(r * row_tile, (r + 1) * row_tile)
        s_scr[rows, :] = _dot(u_ref[rows, :], w_scr[...])

    l16r, l16i = prow[SSM_L]

    def step(xr, xi, sr, si):
        return l16r * xr - l16i * xi + sr, l16r * xi + l16i * xr + si

    def prompt_body(c, carry):
        new = []
        for b in range(n_prompt):
            xr, xi = carry[b]
            row = b * blocks_per_prompt + c
            xst_scr[pl.ds(row, 1), 0:n_state] = xr
            xst_scr[pl.ds(row, 1), n_state:2 * n_state] = xi
            new.append(step(xr, xi, s_scr[pl.ds(row, 1), 0:n_state],
                            s_scr[pl.ds(row, 1), n_state:2 * n_state]))
        return tuple(new)

    zero = jnp.zeros((1, n_state), F32)
    fin = lax.fori_loop(0, blocks_per_prompt, prompt_body, tuple((zero, zero) for _ in range(n_prompt)))
    for b in range(n_prompt):
        xfp_ref[b:b + 1, 0:n_state] = fin[b][0]
        xfp_ref[b:b + 1, n_state:2 * n_state] = fin[b][1]

    base = n_prompt * blocks_per_prompt
    for b in range(n_sample):
        xr = x0_ref[b:b + 1, 0:n_state]
        xi = x0_ref[b:b + 1, n_state:2 * n_state]
        for j in range(blocks_per_sample):
            row = base + b * blocks_per_sample + j
            xst_scr[row:row + 1, 0:n_state] = xr
            xst_scr[row:row + 1, n_state:2 * n_state] = xi
            xr, xi = step(xr, xi, s_scr[row:row + 1, 0:n_state], s_scr[row:row + 1, n_state:2 * n_state])
        xfs_ref[b:b + 1, 0:n_state] = xr
        xfs_ref[b:b + 1, n_state:2 * n_state] = xi

    dvec = d_ref[...]
    for r in range(n_rows // row_tile):
        rows = slice(r * row_tile, (r + 1) * row_tile)
        u = u_ref[rows, :]
        y = _dot(u, toep_scr[...]) + _dot(xst_scr[rows, :].astype(BF16), cmh_scr[:, ch:])
        for t in range(SSM_L):
            cols = slice(t * ch, (t + 1) * ch)
            yt = y[:, cols] + dvec * u[:, cols].astype(F32)
            g_ref[rows, cols] = _gelu_tanh(yt).astype(BF16)


def _ssm_core(u_blk, x0, prow, pcol, bxr, bxi, cxr, cxi, dvec, n_prompt, blocks_per_prompt, n_sample,
              blocks_per_sample, groups_per_tile):
    n_gb, nc, width = u_blk.shape
    ch = LANES
    n_state = bxr.shape[2]
    assert width == SSM_L * ch and nc == n_prompt * blocks_per_prompt + n_sample * blocks_per_sample
    row_tile = _tile(nc, 384)
    kern = functools.partial(_ssm_core_kernel, n_prompt=n_prompt, blocks_per_prompt=blocks_per_prompt,
                             n_sample=n_sample, blocks_per_sample=blocks_per_sample, row_tile=row_tile,
                             groups_per_tile=groups_per_tile)
    return pl.pallas_call(
        kern,
        grid=(n_gb,),
        in_specs=[pl.BlockSpec((None, nc, width), lambda g: (g, 0, 0)),
                  pl.BlockSpec((n_sample, 2 * n_state), lambda g: (0, g)),
                  pl.BlockSpec((None, 3, n_state), lambda g: (g, 0, 0)),
                  pl.BlockSpec((None, n_state, 3), lambda g: (g, 0, 0)),
                  pl.BlockSpec((None, ch, n_state), lambda g: (g, 0, 0)),
                  pl.BlockSpec((None, ch, n_state), lambda g: (g, 0, 0)),
                  pl.BlockSpec((None, n_state, ch), lambda g: (g, 0, 0)),
                  pl.BlockSpec((None, n_state, ch), lambda g: (g, 0, 0)),
                  pl.BlockSpec((1, ch), lambda g: (0, g))],
        out_specs=[pl.BlockSpec((None, nc, width), lambda g: (g, 0, 0)),
                   pl.BlockSpec((n_prompt, 2 * n_state), lambda g: (0, g)),
                   pl.BlockSpec((n_sample, 2 * n_state), lambda g: (0, g))],
        out_shape=[jax.ShapeDtypeStruct((n_gb, nc, width), BF16),
                   jax.ShapeDtypeStruct((n_prompt, n_gb * 2 * n_state), F32),
                   jax.ShapeDtypeStruct((n_sample, n_gb * 2 * n_state), F32)],
        scratch_shapes=[pltpu.VMEM((width, 2 * n_state), BF16),
                        pltpu.VMEM((2 * n_state, width + ch), BF16),
                        pltpu.VMEM((2 * n_state, width), BF16),
                        pltpu.VMEM((width, width), BF16),
                        pltpu.VMEM((nc, 2 * n_state), F32),
                        pltpu.VMEM((nc, 2 * n_state), F32)],
        compiler_params=_params("arbitrary"),
        name="ssm_core",
    )(u_blk, x0, prow, pcol, bxr, bxi, cxr, cxi, dvec)


def _glu_res_kernel(g_ref, x_ref, wa_ref, wb_ref, o_ref):
    g = jnp.concatenate([g_ref[i] for i in range(g_ref.shape[0])], axis=1)
    a = _dot(g, wa_ref[...])
    b = _dot(g, wb_ref[...])
    o_ref[...] = x_ref[...] + a * jax.nn.sigmoid(b)


def _glu_res(x, g_blk, w_a, w_b):
    t, d = x.shape
    n_gb, nc, _ = g_blk.shape
    tc = _tile(nc, 528)
    tn = _tile(d, 1024, LANES)
    out = pl.pallas_call(
        _glu_res_kernel,
        grid=(d // tn, SSM_L, nc // tc),
        in_specs=[pl.BlockSpec((n_gb, tc, LANES), lambda n, s, c: (0, c, s)),
                  pl.BlockSpec((tc, tn), lambda n, s, c: (c, s * (d // tn) + n)),
                  pl.BlockSpec((d, tn), lambda n, s, c: (0, n)),
                  pl.BlockSpec((d, tn), lambda n, s, c: (0, n))],
        out_specs=pl.BlockSpec((tc, tn), lambda n, s, c: (c, s * (d // tn) + n)),
        out_shape=jax.ShapeDtypeStruct((nc, SSM_L * d), F32),
        compiler_params=_params("arbitrary", "arbitrary", "arbitrary"),
        name="glu_res",
    )(g_blk, x.reshape(nc, SSM_L * d), w_a, w_b)
    return out.reshape(t, d)


def _ssm_layouts(a_re, a_im, log_dt, b_re, b_im, c_re, c_im):
    g, p = a_re.shape
    sg = b_re.shape[2]
    gl = LANES // sg
    n_gb = g // gl
    dt = jnp.broadcast_to(log_dt[:, None], (g, p))
    prow = jnp.stack([a_re, a_im, dt], axis=0).reshape(3, n_gb, gl * p).transpose(1, 0, 2)
    pcol = prow.transpose(0, 2, 1)

    def expand_b(b):
        bt = b.reshape(n_gb, gl, p, sg).transpose(0, 3, 1, 2)
        bt = jnp.broadcast_to(bt[:, None], (n_gb, gl, sg, gl, p))
        return bt.reshape(n_gb, gl * sg, gl * p)

    def expand_c(c):
        ct = c.reshape(n_gb, gl, sg, p).transpose(0, 3, 1, 2)
        ct = jnp.broadcast_to(ct[:, None], (n_gb, gl, p, gl, sg))
        return ct.reshape(n_gb, gl * p, gl * sg)

    return prow, pcol, expand_b(b_re), expand_b(b_im), expand_c(c_re), expand_c(c_im)


def _state_to_tiles(st_re, st_im, gl):
    b, g, p = st_re.shape
    n_gb = g // gl
    st = jnp.stack([st_re.reshape(b, n_gb, gl * p), st_im.reshape(b, n_gb, gl * p)], axis=2)
    return st.reshape(b, n_gb * 2 * gl * p)


def _tiles_to_state(x, g, p, gl):
    b = x.shape[0]
    n_gb = g // gl
    x = x.reshape(b, n_gb, 2, gl, p)
    return x[:, :, 0].reshape(b, g, p), x[:, :, 1].reshape(b, g, p)


def kernel(x_prompt, x_sample, cache_attn_k, cache_attn_v, state_ssm_re, state_ssm_im, norm_mix, norm_mlp, norm_final, attn_w_qkv, attn_w_o, attn_rel_bias, ssm_a_re, ssm_a_im, ssm_log_dt, ssm_b_re, ssm_b_im, ssm_c_re, ssm_c_im, ssm_d, ssm_w_glu_a, ssm_w_glu_b, mlp_w_up, mlp_w_down):
    bp, seq, d = x_prompt.shape
    bs, ts, _ = x_sample.shape
    depth = norm_mix.shape[0]
    n_heads, dh = cache_attn_k.shape[3], cache_attn_k.shape[4]
    n_groups, p_state = ssm_a_re.shape[1], ssm_a_re.shape[2]
    sg = d // n_groups
    gl = LANES // sg
    assert ts == CHUNK and seq % KV_TAIL == 0 and (bs * ts) % KV_TAIL == 0
    assert seq % SSM_L == 0 and ts % SSM_L == 0 and d % LANES == 0

    n_prompt_rows = bp * seq
    x = jnp.concatenate([x_prompt.reshape(n_prompt_rows, d), x_sample.reshape(bs * ts, d)], axis=0)
    t_total = x.shape[0]

    k_p, v_p, k_s, v_s = [], [], [], []
    sr_p, si_p, sr_s, si_s = [], [], [], []
    for i in range(depth):
        j = i // 2
        gain = norm_mix[i][None, :]
        if i % 2 == 0:
            w_qkv = attn_w_qkv[j].astype(BF16)
            qkv, tail = _qkv(x, gain, w_qkv, seq // KV_TAIL, bp)
            bias = _bias_table(attn_rel_bias[j])
            o = _attn_prompt(qkv, bias, bp, seq, d, dh, t_total)
            o = _attn_sample(qkv, cache_attn_k[j].reshape(bs, -1, d), cache_attn_v[j].reshape(bs, -1, d),
                             bias, o, n_prompt_rows, d, dh)
            x = _proj_res(x, o, attn_w_o[j].astype(BF16))
            tail_p = tail[:bp * KV_TAIL].reshape(bp, KV_TAIL, 3, n_heads, dh)
            tail_s = tail[bp * KV_TAIL:].reshape(bs, ts, 3, n_heads, dh)
            k_p.append(tail_p[:, :, 1]); v_p.append(tail_p[:, :, 2])
            k_s.append(tail_s[:, :, 1]); v_s.append(tail_s[:, :, 2])
        else:
            layouts = _ssm_layouts(ssm_a_re[j], ssm_a_im[j], ssm_log_dt[j], ssm_b_re[j], ssm_b_im[j],
                                   ssm_c_re[j], ssm_c_im[j])
            u_blk = _norm_blocked(x, gain)
            x0 = _state_to_tiles(state_ssm_re[j], state_ssm_im[j], gl)
            g_blk, xf_p, xf_s = _ssm_core(u_blk, x0, *layouts, ssm_d[j][None, :], bp, seq // SSM_L, bs,
                                          ts // SSM_L, gl)
            x = _glu_res(x, g_blk, ssm_w_glu_a[j].astype(BF16), ssm_w_glu_b[j].astype(BF16))
            rp, ip = _tiles_to_state(xf_p, n_groups, p_state, gl)
            rs, is_ = _tiles_to_state(xf_s, n_groups, p_state, gl)
            sr_p.append(rp); si_p.append(ip); sr_s.append(rs); si_s.append(is_)
        x = _mlp(x, norm_mlp[i][None, :], mlp_w_up[i].astype(BF16), mlp_w_down[i].astype(BF16))

    y_p, y_s = _final_norm(x, norm_final[None, :], n_prompt_rows)
    return (y_p.reshape(bp, seq, d), y_s.reshape(bs, ts, d),
            jnp.stack(k_p), jnp.stack(v_p), jnp.stack(sr_p), jnp.stack(si_p),
            jnp.stack(k_s), jnp.stack(v_s), jnp.stack(sr_s), jnp.stack(si_s))
```

```python
import functools
import math

import jax
import jax.numpy as jnp
from jax import lax
from jax.experimental import pallas as pl
from jax.experimental.pallas import tpu as pltpu

CHUNK = 64
N_PAST_CHUNKS = 8
RMS_EPS = 1e-5
NEG_INF = -1e30

LANES = 128
SUBLANES_BF16 = 16
VMEM_LIMIT_BYTES = 58 * 1024 * 1024

PAIR = 2 * CHUNK
UNIT = 2 * PAIR
KV_TAIL = CHUNK * N_PAST_CHUNKS
WIN = KV_TAIL + UNIT
SSM_L = 16

F32 = jnp.float32
BF16 = jnp.bfloat16


def _params(*sem):
    return pltpu.CompilerParams(dimension_semantics=sem, vmem_limit_bytes=VMEM_LIMIT_BYTES)


def _tile(n, target, mult=SUBLANES_BF16):
    best = None
    for t in range(mult, min(n, target) + 1, mult):
        if n % t == 0:
            best = t
    assert best is not None, (n, target, mult)
    return best


def _rms(x, g):
    y = x * lax.rsqrt(jnp.mean(x * x, axis=-1, keepdims=True) + RMS_EPS)
    return y * g


def _dot(a, b):
    return jnp.dot(a, b, preferred_element_type=F32)


def _dot_nt(a, b):
    return lax.dot_general(a, b, (((1,), (1,)), ((), ())), preferred_element_type=F32)


def _x_specs(xs, tm, d, n_prompt_tiles, grid_rank, row_axis):
    def at(f):
        return lambda *idx: (f(idx[row_axis]), 0)
    if len(xs) == 1:
        return [pl.BlockSpec((tm, d), at(lambda i: i))]
    return [pl.BlockSpec((tm, d), at(lambda i: jnp.minimum(i, n_prompt_tiles - 1))),
            pl.BlockSpec((tm, d), at(lambda i: jnp.maximum(i - n_prompt_tiles, 0)))]


def _read_x(x_refs, i, n_prompt_tiles):
    if len(x_refs) == 1:
        return x_refs[0][...]
    return jnp.where(i < n_prompt_tiles, x_refs[0][...], x_refs[1][...])


def _qkv_kernel(*refs, n_src, n_prompt_tiles):
    x_refs, (g_ref, w_ref, o_ref, tail_ref) = refs[:n_src], refs[n_src:]
    x = _read_x(x_refs, pl.program_id(1), n_prompt_tiles)
    hn = _rms(x, g_ref[...]).astype(BF16)
    acc = _dot(hn, w_ref[...])
    o_ref[...] = acc.astype(BF16)
    tail_ref[...] = acc


def _qkv(xs, gain, w_all, layer, n_prompt_tiles_per_batch, n_prompt_batches):
    d = xs[0].shape[1]
    t = sum(x.shape[0] for x in xs)
    n = w_all.shape[2]
    tm = KV_TAIL
    tn = d
    nt = t // tm
    n_prompt_tiles = n_prompt_tiles_per_batch * n_prompt_batches
    n_slots = n_prompt_batches + (nt - n_prompt_tiles)

    def slot(i):
        return jnp.where(i < n_prompt_tiles, i // n_prompt_tiles_per_batch,
                         i - n_prompt_tiles + n_prompt_batches)

    return pl.pallas_call(
        functools.partial(_qkv_kernel, n_src=len(xs), n_prompt_tiles=n_prompt_tiles),
        grid=(n // tn, nt),
        in_specs=_x_specs(xs, tm, d, n_prompt_tiles, 2, 1) + [
            pl.BlockSpec((1, d), lambda j, i: (0, 0)),
            pl.BlockSpec((None, d, tn), lambda j, i: (layer, 0, j))],
        out_specs=[pl.BlockSpec((tm, tn), lambda j, i: (i, j)),
                   pl.BlockSpec((tm, tn), lambda j, i: (slot(i), j))],
        out_shape=[jax.ShapeDtypeStruct((t, n), BF16),
                   jax.ShapeDtypeStruct((n_slots * tm, n), F32)],
        compiler_params=_params("arbitrary", "arbitrary"),
        name="qkv",
    )(*xs, gain, w_all)


def _proj_res_kernel(*refs, n_src, n_prompt_tiles):
    x_refs, (a_ref, w_ref, o_ref) = refs[:n_src], refs[n_src:]
    x = _read_x(x_refs, pl.program_id(0), n_prompt_tiles)
    o_ref[...] = x + _dot(a_ref[...], w_ref[...])


def _proj_res(xs, a, w_all, layer, n_prompt_rows):
    d = xs[0].shape[1]
    t, k = a.shape
    tm = _tile(math.gcd(n_prompt_rows, t - n_prompt_rows), 512)
    npt = n_prompt_rows // tm
    return pl.pallas_call(
        functools.partial(_proj_res_kernel, n_src=len(xs), n_prompt_tiles=npt),
        grid=(t // tm,),
        in_specs=_x_specs(xs, tm, d, npt, 1, 0) + [
            pl.BlockSpec((tm, k), lambda i: (i, 0)),
            pl.BlockSpec((None, k, d), lambda i: (layer, 0, 0))],
        out_specs=pl.BlockSpec((tm, d), lambda i: (i, 0)),
        out_shape=jax.ShapeDtypeStruct((t, d), F32),
        compiler_params=_params("arbitrary"),
        name="proj_res",
    )(*xs, a, w_all)


def _mlp_kernel(x_ref, g_ref, wu_ref, wd_ref, o_ref, hn_ref):
    k = pl.program_id(1)

    @pl.when(k == 0)
    def _():
        x = x_ref[...]
        hn_ref[...] = _rms(x, g_ref[...]).astype(BF16)
        o_ref[...] = x

    h = jnp.maximum(_dot(hn_ref[...], wu_ref[...]), 0.0)
    o_ref[...] += _dot((h * h).astype(BF16), wd_ref[...])


def _mlp(x, gain, w_up_all, w_down_all, layer):
    t, d = x.shape
    dff = w_up_all.shape[2]
    tm = _tile(t, 768)
    tf = _tile(dff, 1024, LANES)
    return pl.pallas_call(
        _mlp_kernel,
        grid=(t // tm, dff // tf),
        in_specs=[pl.BlockSpec((tm, d), lambda i, k: (i, 0)),
                  pl.BlockSpec((1, d), lambda i, k: (0, 0)),
                  pl.BlockSpec((None, d, tf), lambda i, k: (layer, 0, k)),
                  pl.BlockSpec((None, tf, d), lambda i, k: (layer, k, 0))],
        out_specs=pl.BlockSpec((tm, d), lambda i, k: (i, 0)),
        out_shape=jax.ShapeDtypeStruct((t, d), F32),
        scratch_shapes=[pltpu.VMEM((tm, d), BF16)],
        compiler_params=_params("arbitrary", "arbitrary"),
        name="mlp",
    )(x, gain, w_up_all, w_down_all)


def _final_norm_kernel(x_ref, g_ref, yp_ref, ys_ref, *, n_prompt_tiles):
    i = pl.program_id(0)
    y = _rms(x_ref[...], g_ref[...])

    @pl.when(i < n_prompt_tiles)
    def _():
        yp_ref[...] = y

    @pl.when(i >= n_prompt_tiles)
    def _():
        ys_ref[...] = y


def _final_norm(x, gain, n_prompt_rows):
    t, d = x.shape
    n_sample_rows = t - n_prompt_rows
    tm = _tile(math.gcd(n_prompt_rows, n_sample_rows), 512)
    npt = n_prompt_rows // tm
    return pl.pallas_call(
        functools.partial(_final_norm_kernel, n_prompt_tiles=npt),
        grid=(t // tm,),
        in_specs=[pl.BlockSpec((tm, d), lambda i: (i, 0)),
                  pl.BlockSpec((1, d), lambda i: (0, 0))],
        out_specs=[pl.BlockSpec((tm, d), lambda i: (jnp.minimum(i, npt - 1), 0)),
                   pl.BlockSpec((tm, d), lambda i: (jnp.maximum(i - npt, 0), 0))],
        out_shape=[jax.ShapeDtypeStruct((n_prompt_rows, d), F32),
                   jax.ShapeDtypeStruct((n_sample_rows, d), F32)],
        compiler_params=_params("arbitrary"),
        name="final_norm",
    )(x, gain)


def _bias_kernel(t_ref, o_ref):
    n_heads = o_ref.shape[0]
    width = t_ref.shape[1]
    r = lax.broadcasted_iota(jnp.int32, (UNIT, WIN), 0)
    c = lax.broadcasted_iota(jnp.int32, (UNIT, WIN), 1)
    dist = r // PAIR + KV_TAIL // PAIR - c // PAIR
    q_hi = (r % PAIR) >= CHUNK
    k_hi = (c % PAIR) >= CHUNK
    masked = ((dist < 0) | (dist > KV_TAIL // PAIR)
              | ((dist == 0) & jnp.logical_not(q_hi) & k_hi)
              | ((dist == KV_TAIL // PAIR) & q_hi & jnp.logical_not(k_hi)))
    for h in range(n_heads):
        rows = jnp.broadcast_to(t_ref[h:h + 1, :], (UNIT, width))
        rolled = pltpu.roll(rows, width - (UNIT - 1), 1, stride=1, stride_axis=0)
        o_ref[h] = jnp.where(masked, NEG_INF, rolled[:, :WIN])


def _bias_table(rel_bias):
    n_heads, n_rel = rel_bias.shape
    max_rel = (n_rel - 1) // 2
    assert max_rel == PAIR
    width = WIN + UNIT
    n_far = WIN - 1 - max_rel + 1
    tvec = jnp.concatenate(
        [jnp.broadcast_to(rel_bias[:, 2 * max_rel:], (n_heads, n_far)),
         jnp.flip(rel_bias[:, :2 * max_rel], axis=1),
         jnp.broadcast_to(rel_bias[:, :1], (n_heads, width - n_far - 2 * max_rel))], axis=1)
    return pl.pallas_call(
        _bias_kernel,
        out_shape=jax.ShapeDtypeStruct((n_heads, UNIT, WIN), F32),
        compiler_params=pltpu.CompilerParams(vmem_limit_bytes=VMEM_LIMIT_BYTES),
        name="attn_bias",
    )(tvec)


def _softmax_pv(scores, values):
    m = scores[0].max(axis=1, keepdims=True)
    for s in scores[1:]:
        m = jnp.maximum(m, s.max(axis=1, keepdims=True))
    acc = None
    l = None
    for s, v in zip(scores, values):
        p = jnp.exp(s - m)
        ls = p.sum(axis=1, keepdims=True)
        pv = _dot(p.astype(BF16), v)
        acc = pv if acc is None else acc + pv
        l = ls if l is None else l + ls
    return acc / l


def _attn_prompt_kernel(q_ref, k_ref, v_ref, b_ref, o_ref, *, n_heads, dh, n_units):
    qi = pl.program_id(2)
    scale = dh ** -0.5

    def run(items):
        def raw_scores(item):
            q_rows, h, key_rows, _ = item
            hs = slice(h * dh, (h + 1) * dh)
            return _dot_nt(q_ref[q_rows, hs], k_ref[key_rows, hs])

        s_next = raw_scores(items[0])
        for n, item in enumerate(items):
            s_raw = s_next
            if n + 1 < len(items):
                s_next = raw_scores(items[n + 1])
            q_rows, h, key_rows, b_cols = item
            hs = slice(h * dh, (h + 1) * dh)
            o = _softmax_pv([s_raw * scale + b_ref[h, :, b_cols]], [v_ref[key_rows, hs]])
            o_ref[q_rows, hs] = o.astype(BF16)

    @pl.when(qi == 0)
    def _():
        items = []
        for u in range(n_units):
            nk = (u + 1) * UNIT
            for h in range(n_heads):
                items.append((slice(u * UNIT, (u + 1) * UNIT), h, slice(0, nk), slice(WIN - nk, WIN)))
        run(items)

    @pl.when(qi > 0)
    def _():
        items = []
        for u in range(n_units):
            k0 = pl.multiple_of((qi * n_units + u) * UNIT - KV_TAIL, PAIR)
            for h in range(n_heads):
                items.append((slice(u * UNIT, (u + 1) * UNIT), h, pl.ds(k0, WIN), slice(0, WIN)))
        run(items)


def _attn_prompt(qkv, bias, n_batches, seq, d, dh, t_total):
    n_units = KV_TAIL // UNIT
    tq = n_units * UNIT
    assert seq % tq == 0 and seq >= WIN
    hw = min(d, 4 * dh)
    n_hg = d // hw
    n_heads = hw // dh
    return pl.pallas_call(
        functools.partial(_attn_prompt_kernel, n_heads=n_heads, dh=dh, n_units=n_units),
        grid=(n_batches, n_hg, seq // tq),
        in_specs=[pl.BlockSpec((tq, hw), lambda b, g, i: (b * (seq // tq) + i, g)),
                  pl.BlockSpec((seq, hw), lambda b, g, i: (b, n_hg + g)),
                  pl.BlockSpec((seq, hw), lambda b, g, i: (b, 2 * n_hg + g)),
                  pl.BlockSpec((n_heads, UNIT, WIN), lambda b, g, i: (g, 0, 0))],
        out_specs=pl.BlockSpec((tq, hw), lambda b, g, i: (b * (seq // tq) + i, g)),
        out_shape=jax.ShapeDtypeStruct((t_total, d), BF16),
        compiler_params=_params("arbitrary", "arbitrary", "arbitrary"),
        name="attn_prompt",
    )(qkv, qkv, qkv, bias)


def _attn_sample_kernel(q_ref, kn_ref, vn_ref, kc_ref, vc_ref, b_ref, oin_ref, o_ref, *, n_heads, dh):
    del oin_ref
    scale = dh ** -0.5
    ts = q_ref.shape[0]
    lc = kc_ref.shape[0] // n_heads

    def cached(ref, h):
        return ref[pl.ds(h, lc, stride=n_heads), :].astype(BF16)

    def raw_scores(h):
        hs = slice(h * dh, (h + 1) * dh)
        q = q_ref[:, hs]
        return _dot_nt(q, cached(kc_ref, h)), _dot_nt(q, kn_ref[:, hs])

    s_next = raw_scores(0)
    for h in range(n_heads):
        s_c, s_n = s_next
        if h + 1 < n_heads:
            s_next = raw_scores(h + 1)
        hs = slice(h * dh, (h + 1) * dh)
        o = _softmax_pv([s_c * scale + b_ref[h, :ts, :lc], s_n * scale + b_ref[h, :ts, lc:lc + ts]],
                        [cached(vc_ref, h), vn_ref[:, hs]])
        o_ref[:, hs] = o.astype(BF16)


def _attn_sample(qkv, k_cache, v_cache, layer, bias, o_all, row0, d, dh):
    _, n_b, lc, n_heads, _ = k_cache.shape
    ts = CHUNK
    assert lc == KV_TAIL and row0 % ts == 0 and n_heads * dh == d
    r0 = row0 // ts
    cache_shape = (k_cache.shape[0], n_b, lc * n_heads, dh)
    cache_spec = pl.BlockSpec((None, None, lc * n_heads, dh), lambda b: (layer, b, 0, 0))
    return pl.pallas_call(
        functools.partial(_attn_sample_kernel, n_heads=n_heads, dh=dh),
        grid=(n_b,),
        in_specs=[pl.BlockSpec((ts, d), lambda b: (r0 + b, 0)),
                  pl.BlockSpec((ts, d), lambda b: (r0 + b, 1)),
                  pl.BlockSpec((ts, d), lambda b: (r0 + b, 2)),
                  cache_spec, cache_spec,
                  pl.BlockSpec((n_heads, UNIT, WIN), lambda b: (0, 0, 0)),
                  pl.BlockSpec(memory_space=pl.ANY)],
        out_specs=pl.BlockSpec((ts, d), lambda b: (r0 + b, 0)),
        out_shape=jax.ShapeDtypeStruct(o_all.shape, o_all.dtype),
        input_output_aliases={6: 0},
        compiler_params=_params("arbitrary"),
        name="attn_sample",
    )(qkv, qkv, qkv, k_cache.reshape(cache_shape), v_cache.reshape(cache_shape), bias, o_all)


def _norm_blocked_kernel(x_ref, g_ref, o_ref, slab_scr):
    n_gb = o_ref.shape[0]
    tm = x_ref.shape[0]
    hn = _rms(x_ref[...], g_ref[...])
    for gb in range(n_gb):
        slab_scr[gb * tm:(gb + 1) * tm, :] = hn[:, gb * LANES:(gb + 1) * LANES]
    for gb in range(n_gb):
        for s in range(SSM_L):
            rows = pl.ds(gb * tm + s, tm // SSM_L, stride=SSM_L)
            o_ref[gb, :, s * LANES:(s + 1) * LANES] = slab_scr[rows, :].astype(BF16)


def _norm_blocked(x, gain):
    t, d = x.shape
    nc = t // SSM_L
    tm = _tile(t, 768, SSM_L * SUBLANES_BF16)
    n_gb = d // LANES
    return pl.pallas_call(
        _norm_blocked_kernel,
        grid=(t // tm,),
        in_specs=[pl.BlockSpec((tm, d), lambda i: (i, 0)),
                  pl.BlockSpec((1, d), lambda i: (0, 0))],
        out_specs=pl.BlockSpec((n_gb, tm // SSM_L, SSM_L * LANES), lambda i: (0, i, 0)),
        out_shape=jax.ShapeDtypeStruct((n_gb, nc, SSM_L * LANES), BF16),
        scratch_shapes=[pltpu.VMEM((n_gb * tm, LANES), F32)],
        compiler_params=_params("arbitrary"),
        name="ssm_norm",
    )(x, gain)


def _gelu_tanh(x):
    cdf = 0.5 * (1.0 + jnp.tanh(math.sqrt(2.0 / math.pi) * (x + 0.044715 * (x ** 3))))
    return x * cdf


def _split_bf16(x):
    hi = x.astype(BF16)
    lo = (x - hi.astype(F32)).astype(BF16)
    return hi, lo


def _ssm_core_kernel(u_ref, x0_ref, prow_ref, pcol_ref, bxr_ref, bxi_ref, cxr_ref, cxi_ref, d_ref,
                     g_ref, xfp_ref, xfs_ref,
                     w_scr, cmh_scr, cml_scr, toep_scr, s_scr, xst_scr,
                     *, n_prompt, blocks_per_prompt, n_sample, blocks_per_sample, row_tile,
                     groups_per_tile):
    n_state = bxr_ref.shape[1]
    ch = bxr_ref.shape[0]

    def discretise(a_re, a_im, log_dt):
        dt = jnp.exp(log_dt)
        mag = jnp.exp(dt * a_re)
        l_re = mag * jnp.cos(dt * a_im)
        l_im = mag * jnp.sin(dt * a_im)
        den = a_re * a_re + a_im * a_im
        n_re = l_re - 1.0
        n_im = l_im
        z_re = (n_re * a_re + n_im * a_im) / den
        z_im = (n_im * a_re - n_re * a_im) / den
        return l_re, l_im, z_re, z_im

    def powers(l_re, l_im, n):
        out = [(jnp.ones_like(l_re), jnp.zeros_like(l_im))]
        for _ in range(n):
            pr, pi = out[-1]
            out.append((pr * l_re - pi * l_im, pr * l_im + pi * l_re))
        return out

    lr, li, zr, zi = discretise(prow_ref[0:1, :], prow_ref[1:2, :], prow_ref[2:3, :])
    prow = powers(lr, li, SSM_L)
    lcr, lci, _, _ = discretise(pcol_ref[:, 0:1], pcol_ref[:, 1:2], pcol_ref[:, 2:3])
    pcol = powers(lcr, lci, SSM_L)

    p = n_state // groups_per_tile
    sg = ch // groups_per_tile
    mb = (lax.broadcasted_iota(jnp.int32, (ch, n_state), 0) // sg ==
          lax.broadcasted_iota(jnp.int32, (ch, n_state), 1) // p)
    mc = (lax.broadcasted_iota(jnp.int32, (n_state, ch), 0) // p ==
          lax.broadcasted_iota(jnp.int32, (n_state, ch), 1) // sg)

    bxr = bxr_ref[...]
    bxi = bxi_ref[...]
    bb_r = jnp.where(mb, zr * bxr - zi * bxi, 0.0)
    bb_i = jnp.where(mb, zr * bxi + zi * bxr, 0.0)

    for s in range(SSM_L):
        pr, pi = prow[SSM_L - 1 - s]
        rows = slice(s * ch, (s + 1) * ch)
        w_scr[rows, 0:n_state] = (pr * bb_r - pi * bb_i).astype(BF16)
        w_scr[rows, n_state:2 * n_state] = (pr * bb_i + pi * bb_r).astype(BF16)

    cxr = cxr_ref[...]
    cxi = cxi_ref[...]
    for tau in range(SSM_L + 1):
        pr, pi = pcol[tau]
        m_re = jnp.where(mc, cxr * pr - cxi * pi, 0.0)
        m_im = jnp.where(mc, cxr * pi + cxi * pr, 0.0)
        cols = slice(tau * ch, (tau + 1) * ch)
        hi_r, lo_r = _split_bf16(m_re)
        hi_i, lo_i = _split_bf16(-m_im)
        cmh_scr[0:n_state, cols] = hi_r
        cmh_scr[n_state:2 * n_state, cols] = hi_i
        if tau < SSM_L:
            cml_scr[0:n_state, cols] = lo_r
            cml_scr[n_state:2 * n_state, cols] = lo_i

    a_hi_r, a_lo_r = _split_bf16(bb_r)
    a_hi_i, a_lo_i = _split_bf16(bb_i)
    a_hi = jnp.concatenate([a_hi_r, a_hi_i], axis=1)
    a_lo = jnp.concatenate([a_lo_r, a_lo_i], axis=1)
    width = SSM_L * ch
    k_all = (_dot(a_hi, cmh_scr[:, 0:width]) + _dot(a_lo, cmh_scr[:, 0:width])
             + _dot(a_hi, cml_scr[...]))
    k_all = k_all.astype(BF16)
    for s in range(SSM_L):
        rows = slice(s * ch, (s + 1) * ch)
        if s % 2 == 1:
            toep_scr[rows, (s - 1) * ch:s * ch] = jnp.zeros((ch, ch), BF16)
        toep_scr[rows, s * ch:width] = k_all[:, 0:width - s * ch]

    n_rows = u_ref.shape[0]
    for r in range(n_rows // row_tile):
        rows = slice(r * row_tile, (r + 1) * row_tile)
        s_scr[rows, :] = _dot(u_ref[rows, :], w_scr[...])

    l16r, l16i = prow[SSM_L]

    def step(xr, xi, sr, si):
        return l16r * xr - l16i * xi + sr, l16r * xi + l16i * xr + si

    def prompt_body(c, carry):
        new = []
        for b in range(n_prompt):
            xr, xi = carry[b]
            row = b * blocks_per_prompt + c
            xst_scr[pl.ds(row, 1), 0:n_state] = xr
            xst_scr[pl.ds(row, 1), n_state:2 * n_state] = xi
            new.append(step(xr, xi, s_scr[pl.ds(row, 1), 0:n_state],
                            s_scr[pl.ds(row, 1), n_state:2 * n_state]))
        return tuple(new)

    zero = jnp.zeros((1, n_state), F32)
    fin = lax.fori_loop(0, blocks_per_prompt, prompt_body, tuple((zero, zero) for _ in range(n_prompt)))
    for b in range(n_prompt):
        xfp_ref[b:b + 1, 0:n_state] = fin[b][0]
        xfp_ref[b:b + 1, n_state:2 * n_state] = fin[b][1]

    base = n_prompt * blocks_per_prompt
    for b in range(n_sample):
        xr = x0_ref[b:b + 1, 0:n_state]
        xi = x0_ref[b:b + 1, n_state:2 * n_state]
        for j in range(blocks_per_sample):
            row = base + b * blocks_per_sample + j
            xst_scr[row:row + 1, 0:n_state] = xr
            xst_scr[row:row + 1, n_state:2 * n_state] = xi
            xr, xi = step(xr, xi, s_scr[row:row + 1, 0:n_state], s_scr[row:row + 1, n_state:2 * n_state])
        xfs_ref[b:b + 1, 0:n_state] = xr
        xfs_ref[b:b + 1, n_state:2 * n_state] = xi

    dvec = d_ref[...]
    for r in range(n_rows // row_tile):
        rows = slice(r * row_tile, (r + 1) * row_tile)
        xst = xst_scr[rows, :].astype(BF16)
        for t in range(0, SSM_L, 2):
            k_ext = (t + 2) * ch
            y = (_dot(u_ref[rows, 0:k_ext], toep_scr[0:k_ext, t * ch:(t + 2) * ch])
                 + _dot(xst, cmh_scr[:, (t + 1) * ch:(t + 3) * ch]))
            for tt in (t, t + 1):
                cols = slice(tt * ch, (tt + 1) * ch)
                yt = y[:, (tt - t) * ch:(tt - t + 1) * ch] + dvec * u_ref[rows, cols].astype(F32)
                g_ref[rows, cols] = _gelu_tanh(yt).astype(BF16)


def _ssm_core(u_blk, x0, prow, pcol, bxr, bxi, cxr, cxi, dvec, n_prompt, blocks_per_prompt, n_sample,
              blocks_per_sample, groups_per_tile):
    n_gb, nc, width = u_blk.shape
    ch = LANES
    n_state = bxr.shape[2]
    assert width == SSM_L * ch and nc == n_prompt * blocks_per_prompt + n_sample * blocks_per_sample
    row_tile = _tile(nc, 384)
    kern = functools.partial(_ssm_core_kernel, n_prompt=n_prompt, blocks_per_prompt=blocks_per_prompt,
                             n_sample=n_sample, blocks_per_sample=blocks_per_sample, row_tile=row_tile,
                             groups_per_tile=groups_per_tile)
    return pl.pallas_call(
        kern,
        grid=(n_gb,),
        in_specs=[pl.BlockSpec((None, nc, width), lambda g: (g, 0, 0)),
                  pl.BlockSpec((n_sample, 2 * n_state), lambda g: (0, g)),
                  pl.BlockSpec((None, 3, n_state), lambda g: (g, 0, 0)),
                  pl.BlockSpec((None, n_state, 3), lambda g: (g, 0, 0)),
                  pl.BlockSpec((None, ch, n_state), lambda g: (g, 0, 0)),
                  pl.BlockSpec((None, ch, n_state), lambda g: (g, 0, 0)),
                  pl.BlockSpec((None, n_state, ch), lambda g: (g, 0, 0)),
                  pl.BlockSpec((None, n_state, ch), lambda g: (g, 0, 0)),
                  pl.BlockSpec((1, ch), lambda g: (0, g))],
        out_specs=[pl.BlockSpec((None, nc, width), lambda g: (g, 0, 0)),
                   pl.BlockSpec((n_prompt, 2 * n_state), lambda g: (0, g)),
                   pl.BlockSpec((n_sample, 2 * n_state), lambda g: (0, g))],
        out_shape=[jax.ShapeDtypeStruct((n_gb, nc, width), BF16),
                   jax.ShapeDtypeStruct((n_prompt, n_gb * 2 * n_state), F32),
                   jax.ShapeDtypeStruct((n_sample, n_gb * 2 * n_state), F32)],
        scratch_shapes=[pltpu.VMEM((width, 2 * n_state), BF16),
                        pltpu.VMEM((2 * n_state, width + ch), BF16),
                        pltpu.VMEM((2 * n_state, width), BF16),
                        pltpu.VMEM((width, width), BF16),
                        pltpu.VMEM((nc, 2 * n_state), F32),
                        pltpu.VMEM((nc, 2 * n_state), F32)],
        compiler_params=_params("arbitrary"),
        name="ssm_core",
    )(u_blk, x0, prow, pcol, bxr, bxi, cxr, cxi, dvec)


def _glu_res_kernel(g_ref, x_ref, wa_ref, wb_ref, o_ref, slab_scr, lhs_scr):
    n_gb = g_ref.shape[0]
    tm = x_ref.shape[0]

    @pl.when(pl.program_id(1) == 0)
    def _():
        for gb in range(n_gb):
            for s in range(SSM_L):
                rows = pl.ds(gb * tm + s, tm // SSM_L, stride=SSM_L)
                slab_scr[rows, :] = g_ref[gb, :, s * LANES:(s + 1) * LANES].astype(F32)
        for gb in range(n_gb):
            lhs_scr[:, gb * LANES:(gb + 1) * LANES] = slab_scr[gb * tm:(gb + 1) * tm, :].astype(BF16)

    g = lhs_scr[...]
    a = _dot(g, wa_ref[...])
    b = _dot(g, wb_ref[...])
    o_ref[...] = x_ref[...] + a * jax.nn.sigmoid(b)


def _glu_res(x, g_blk, w_a_all, w_b_all, layer):
    t, d = x.shape
    n_gb, nc, _ = g_blk.shape
    tm = _tile(t, 768, SSM_L * SUBLANES_BF16)
    tn = _tile(d, 1024, LANES)
    return pl.pallas_call(
        _glu_res_kernel,
        grid=(t // tm, d // tn),
        in_specs=[pl.BlockSpec((n_gb, tm // SSM_L, SSM_L * LANES), lambda i, n: (0, i, 0)),
                  pl.BlockSpec((tm, tn), lambda i, n: (i, n)),
                  pl.BlockSpec((None, d, tn), lambda i, n: (layer, 0, n)),
                  pl.BlockSpec((None, d, tn), lambda i, n: (layer, 0, n))],
        out_specs=pl.BlockSpec((tm, tn), lambda i, n: (i, n)),
        out_shape=jax.ShapeDtypeStruct((t, d), F32),
        scratch_shapes=[pltpu.VMEM((n_gb * tm, LANES), F32),
                        pltpu.VMEM((tm, d), BF16)],
        compiler_params=_params("arbitrary", "arbitrary"),
        name="glu_res",
    )(g_blk, x, w_a_all, w_b_all)


def _ssm_layouts(a_re, a_im, log_dt, b_re, b_im, c_re, c_im):
    g, p = a_re.shape
    sg = b_re.shape[2]
    gl = LANES // sg
    n_gb = g // gl
    dt = jnp.broadcast_to(log_dt[:, None], (g, p))
    prow = jnp.stack([a_re, a_im, dt], axis=0).reshape(3, n_gb, gl * p).transpose(1, 0, 2)
    pcol = prow.transpose(0, 2, 1)

    def expand_b(b):
        bt = b.reshape(n_gb, gl, p, sg).transpose(0, 3, 1, 2)
        bt = jnp.broadcast_to(bt[:, None], (n_gb, gl, sg, gl, p))
        return bt.reshape(n_gb, gl * sg, gl * p)

    def expand_c(c):
        ct = c.reshape(n_gb, gl, sg, p).transpose(0, 3, 1, 2)
        ct = jnp.broadcast_to(ct[:, None], (n_gb, gl, p, gl, sg))
        return ct.reshape(n_gb, gl * p, gl * sg)

    return prow, pcol, expand_b(b_re), expand_b(b_im), expand_c(c_re), expand_c(c_im)


def _state_to_tiles(st_re, st_im, gl):
    b, g, p = st_re.shape
    n_gb = g // gl
    st = jnp.stack([st_re.reshape(b, n_gb, gl * p), st_im.reshape(b, n_gb, gl * p)], axis=2)
    return st.reshape(b, n_gb * 2 * gl * p)


def _tiles_to_state(x, g, p, gl):
    b = x.shape[0]
    n_gb = g // gl
    x = x.reshape(b, n_gb, 2, gl, p)
    return x[:, :, 0].reshape(b, g, p), x[:, :, 1].reshape(b, g, p)


def kernel(x_prompt, x_sample, cache_attn_k, cache_attn_v, state_ssm_re, state_ssm_im, norm_mix, norm_mlp, norm_final, attn_w_qkv, attn_w_o, attn_rel_bias, ssm_a_re, ssm_a_im, ssm_log_dt, ssm_b_re, ssm_b_im, ssm_c_re, ssm_c_im, ssm_d, ssm_w_glu_a, ssm_w_glu_b, mlp_w_up, mlp_w_down):
    bp, seq, d = x_prompt.shape
    bs, ts, _ = x_sample.shape
    depth = norm_mix.shape[0]
    n_heads, dh = cache_attn_k.shape[3], cache_attn_k.shape[4]
    n_groups, p_state = ssm_a_re.shape[1], ssm_a_re.shape[2]
    sg = d // n_groups
    gl = LANES // sg
    assert ts == CHUNK and seq % KV_TAIL == 0 and (bs * ts) % KV_TAIL == 0
    assert seq % SSM_L == 0 and ts % SSM_L == 0 and d % LANES == 0

    n_prompt_rows = bp * seq
    t_total = n_prompt_rows + bs * ts
    xs = (x_prompt.reshape(n_prompt_rows, d), x_sample.reshape(bs * ts, d))

    w_qkv = attn_w_qkv.astype(BF16)
    w_o = attn_w_o.astype(BF16)
    w_glu_a = ssm_w_glu_a.astype(BF16)
    w_glu_b = ssm_w_glu_b.astype(BF16)
    w_up = mlp_w_up.astype(BF16)
    w_down = mlp_w_down.astype(BF16)

    k_p, v_p, k_s, v_s = [], [], [], []
    sr_p, si_p, sr_s, si_s = [], [], [], []
    for i in range(depth):
        j = i // 2
        gain = norm_mix[i][None, :]
        if i % 2 == 0:
            qkv, tail = _qkv(xs, gain, w_qkv, j, seq // KV_TAIL, bp)
            bias = _bias_table(attn_rel_bias[j])
            o = _attn_prompt(qkv, bias, bp, seq, d, dh, t_total)
            o = _attn_sample(qkv, cache_attn_k, cache_attn_v, j, bias, o, n_prompt_rows, d, dh)
            x = _proj_res(xs, o, w_o, j, n_prompt_rows)
            tail_p = tail[:bp * KV_TAIL].reshape(bp, KV_TAIL, 3, n_heads, dh)
            tail_s = tail[bp * KV_TAIL:].reshape(bs, ts, 3, n_heads, dh)
            k_p.append(tail_p[:, :, 1]); v_p.append(tail_p[:, :, 2])
            k_s.append(tail_s[:, :, 1]); v_s.append(tail_s[:, :, 2])
        else:
            x = xs[0]
            layouts = _ssm_layouts(ssm_a_re[j], ssm_a_im[j], ssm_log_dt[j], ssm_b_re[j], ssm_b_im[j],
                                   ssm_c_re[j], ssm_c_im[j])
            u_blk = _norm_blocked(x, gain)
            x0 = _state_to_tiles(state_ssm_re[j], state_ssm_im[j], gl)
            g_blk, xf_p, xf_s = _ssm_core(u_blk, x0, *layouts, ssm_d[j][None, :], bp, seq // SSM_L, bs,
                                          ts // SSM_L, gl)
            x = _glu_res(x, g_blk, w_glu_a, w_glu_b, j)
            rp, ip = _tiles_to_state(xf_p, n_groups, p_state, gl)
            rs, is_ = _tiles_to_state(xf_s, n_groups, p_state, gl)
            sr_p.append(rp); si_p.append(ip); sr_s.append(rs); si_s.append(is_)
        x = _mlp(x, norm_mlp[i][None, :], w_up, w_down, i)
        xs = (x,)

    y_p, y_s = _final_norm(xs[0], norm_final[None, :], n_prompt_rows)
    return (y_p.reshape(bp, seq, d), y_s.reshape(bs, ts, d),
            jnp.stack(k_p), jnp.stack(v_p), jnp.stack(sr_p), jnp.stack(si_p),
            jnp.stack(k_s), jnp.stack(v_s), jnp.stack(sr_s), jnp.stack(si_s))
```

```python
import functools
import math

import jax
import jax.numpy as jnp
from jax import lax
from jax.experimental import pallas as pl
from jax.experimental.pallas import tpu as pltpu

CHUNK = 64
N_PAST_CHUNKS = 8
RMS_EPS = 1e-5
NEG_INF = -1e30

LANES = 128
SUBLANES_BF16 = 16
VMEM_LIMIT_BYTES = 58 * 1024 * 1024

PAIR = 2 * CHUNK
UNIT = 2 * PAIR
KV_TAIL = CHUNK * N_PAST_CHUNKS
WIN = KV_TAIL + UNIT
SSM_L = 16

F32 = jnp.float32
BF16 = jnp.bfloat16


def _params(*sem):
    return pltpu.CompilerParams(dimension_semantics=sem, vmem_limit_bytes=VMEM_LIMIT_BYTES)


def _tile(n, target, mult=SUBLANES_BF16):
    best = None
    for t in range(mult, min(n, target) + 1, mult):
        if n % t == 0:
            best = t
    assert best is not None, (n, target, mult)
    return best


def _rms(x, g):
    y = x * lax.rsqrt(jnp.mean(x * x, axis=-1, keepdims=True) + RMS_EPS)
    return y * g


def _dot(a, b):
    return jnp.dot(a, b, preferred_element_type=F32)


def _dot_nt(a, b):
    return lax.dot_general(a, b, (((1,), (1,)), ((), ())), preferred_element_type=F32)


def _x_specs(xs, tm, d, n_prompt_tiles, grid_rank, row_axis):
    def at(f):
        return lambda *idx: (f(idx[row_axis]), 0)
    if len(xs) == 1:
        return [pl.BlockSpec((tm, d), at(lambda i: i))]
    return [pl.BlockSpec((tm, d), at(lambda i: jnp.minimum(i, n_prompt_tiles - 1))),
            pl.BlockSpec((tm, d), at(lambda i: jnp.maximum(i - n_prompt_tiles, 0)))]


def _read_x(x_refs, i, n_prompt_tiles):
    if len(x_refs) == 1:
        return x_refs[0][...]
    return jnp.where(i < n_prompt_tiles, x_refs[0][...], x_refs[1][...])


def _qkv_kernel(*refs, n_src, n_prompt_tiles):
    x_refs, (g_ref, w_ref, o_ref, tail_ref) = refs[:n_src], refs[n_src:]
    x = _read_x(x_refs, pl.program_id(1), n_prompt_tiles)
    hn = _rms(x, g_ref[...]).astype(BF16)
    acc = _dot(hn, w_ref[...])
    o_ref[...] = acc.astype(BF16)
    tail_ref[...] = acc


def _qkv(xs, gain, w_all, layer, n_prompt_tiles_per_batch, n_prompt_batches):
    d = xs[0].shape[1]
    t = sum(x.shape[0] for x in xs)
    n = w_all.shape[2]
    tm = KV_TAIL
    tn = d
    nt = t // tm
    n_prompt_tiles = n_prompt_tiles_per_batch * n_prompt_batches
    n_slots = n_prompt_batches + (nt - n_prompt_tiles)

    def slot(i):
        return jnp.where(i < n_prompt_tiles, i // n_prompt_tiles_per_batch,
                         i - n_prompt_tiles + n_prompt_batches)

    return pl.pallas_call(
        functools.partial(_qkv_kernel, n_src=len(xs), n_prompt_tiles=n_prompt_tiles),
        grid=(n // tn, nt),
        in_specs=_x_specs(xs, tm, d, n_prompt_tiles, 2, 1) + [
            pl.BlockSpec((1, d), lambda j, i: (0, 0)),
            pl.BlockSpec((None, d, tn), lambda j, i: (layer, 0, j))],
        out_specs=[pl.BlockSpec((tm, tn), lambda j, i: (i, j)),
                   pl.BlockSpec((tm, tn), lambda j, i: (slot(i), j))],
        out_shape=[jax.ShapeDtypeStruct((t, n), BF16),
                   jax.ShapeDtypeStruct((n_slots * tm, n), F32)],
        compiler_params=_params("arbitrary", "arbitrary"),
        name="qkv",
    )(*xs, gain, w_all)


def _proj_res_kernel(*refs, n_src, n_prompt_tiles):
    x_refs, (a_ref, w_ref, o_ref) = refs[:n_src], refs[n_src:]
    x = _read_x(x_refs, pl.program_id(0), n_prompt_tiles)
    o_ref[...] = x + _dot(a_ref[...], w_ref[...])


def _proj_res(xs, a, w_all, layer, n_prompt_rows):
    d = xs[0].shape[1]
    t, k = a.shape
    tm = _tile(math.gcd(n_prompt_rows, t - n_prompt_rows), 512)
    npt = n_prompt_rows // tm
    return pl.pallas_call(
        functools.partial(_proj_res_kernel, n_src=len(xs), n_prompt_tiles=npt),
        grid=(t // tm,),
        in_specs=_x_specs(xs, tm, d, npt, 1, 0) + [
            pl.BlockSpec((tm, k), lambda i: (i, 0)),
            pl.BlockSpec((None, k, d), lambda i: (layer, 0, 0), pipeline_mode=pl.Buffered(1))],
        out_specs=pl.BlockSpec((tm, d), lambda i: (i, 0)),
        out_shape=jax.ShapeDtypeStruct((t, d), F32),
        compiler_params=_params("arbitrary"),
        name="proj_res",
    )(*xs, a, w_all)


def _mlp_kernel(x_ref, g_ref, wu_ref, wd_ref, o_ref, hn_ref):
    k = pl.program_id(1)

    @pl.when(k == 0)
    def _():
        x = x_ref[...]
        hn_ref[...] = _rms(x, g_ref[...]).astype(BF16)
        o_ref[...] = x

    h = jnp.maximum(_dot(hn_ref[...], wu_ref[...]), 0.0)
    o_ref[...] += _dot((h * h).astype(BF16), wd_ref[...])


def _mlp(x, gain, w_up_all, w_down_all, layer):
    t, d = x.shape
    dff = w_up_all.shape[2]
    tm = _tile(t, 768)
    tf = _tile(dff, 1024, LANES)
    return pl.pallas_call(
        _mlp_kernel,
        grid=(t // tm, dff // tf),
        in_specs=[pl.BlockSpec((tm, d), lambda i, k: (i, 0)),
                  pl.BlockSpec((1, d), lambda i, k: (0, 0)),
                  pl.BlockSpec((None, d, tf), lambda i, k: (layer, 0, k)),
                  pl.BlockSpec((None, tf, d), lambda i, k: (layer, k, 0))],
        out_specs=pl.BlockSpec((tm, d), lambda i, k: (i, 0)),
        out_shape=jax.ShapeDtypeStruct((t, d), F32),
        scratch_shapes=[pltpu.VMEM((tm, d), BF16)],
        compiler_params=_params("arbitrary", "arbitrary"),
        name="mlp",
    )(x, gain, w_up_all, w_down_all)


def _final_norm_kernel(x_ref, g_ref, yp_ref, ys_ref, *, n_prompt_tiles):
    i = pl.program_id(0)
    y = _rms(x_ref[...], g_ref[...])

    @pl.when(i < n_prompt_tiles)
    def _():
        yp_ref[...] = y

    @pl.when(i >= n_prompt_tiles)
    def _():
        ys_ref[...] = y


def _final_norm(x, gain, n_prompt_rows):
    t, d = x.shape
    n_sample_rows = t - n_prompt_rows
    tm = _tile(math.gcd(n_prompt_rows, n_sample_rows), 512)
    npt = n_prompt_rows // tm
    return pl.pallas_call(
        functools.partial(_final_norm_kernel, n_prompt_tiles=npt),
        grid=(t // tm,),
        in_specs=[pl.BlockSpec((tm, d), lambda i: (i, 0)),
                  pl.BlockSpec((1, d), lambda i: (0, 0))],
        out_specs=[pl.BlockSpec((tm, d), lambda i: (jnp.minimum(i, npt - 1), 0)),
                   pl.BlockSpec((tm, d), lambda i: (jnp.maximum(i - npt, 0), 0))],
        out_shape=[jax.ShapeDtypeStruct((n_prompt_rows, d), F32),
                   jax.ShapeDtypeStruct((n_sample_rows, d), F32)],
        compiler_params=_params("arbitrary"),
        name="final_norm",
    )(x, gain)


def _bias_kernel(t_ref, o_ref):
    n_heads = o_ref.shape[0]
    width = t_ref.shape[1]
    r = lax.broadcasted_iota(jnp.int32, (UNIT, WIN), 0)
    c = lax.broadcasted_iota(jnp.int32, (UNIT, WIN), 1)
    dist = r // PAIR + KV_TAIL // PAIR - c // PAIR
    q_hi = (r % PAIR) >= CHUNK
    k_hi = (c % PAIR) >= CHUNK
    masked = ((dist < 0) | (dist > KV_TAIL // PAIR)
              | ((dist == 0) & jnp.logical_not(q_hi) & k_hi)
              | ((dist == KV_TAIL // PAIR) & q_hi & jnp.logical_not(k_hi)))
    for h in range(n_heads):
        rows = jnp.broadcast_to(t_ref[h:h + 1, :], (UNIT, width))
        rolled = pltpu.roll(rows, width - (UNIT - 1), 1, stride=1, stride_axis=0)
        o_ref[h] = jnp.where(masked, NEG_INF, rolled[:, :WIN])


def _bias_table(rel_bias):
    n_heads, n_rel = rel_bias.shape
    max_rel = (n_rel - 1) // 2
    assert max_rel == PAIR
    width = WIN + UNIT
    n_far = WIN - 1 - max_rel + 1
    tvec = jnp.concatenate(
        [jnp.broadcast_to(rel_bias[:, 2 * max_rel:], (n_heads, n_far)),
         jnp.flip(rel_bias[:, :2 * max_rel], axis=1),
         jnp.broadcast_to(rel_bias[:, :1], (n_heads, width - n_far - 2 * max_rel))], axis=1)
    return pl.pallas_call(
        _bias_kernel,
        out_shape=jax.ShapeDtypeStruct((n_heads, UNIT, WIN), F32),
        compiler_params=pltpu.CompilerParams(vmem_limit_bytes=VMEM_LIMIT_BYTES),
        name="attn_bias",
    )(tvec)


def _softmax_pv(scores, values):
    m = scores[0].max(axis=1, keepdims=True)
    for s in scores[1:]:
        m = jnp.maximum(m, s.max(axis=1, keepdims=True))
    acc = None
    l = None
    for s, v in zip(scores, values):
        p = jnp.exp(s - m)
        ls = p.sum(axis=1, keepdims=True)
        pv = _dot(p.astype(BF16), v)
        acc = pv if acc is None else acc + pv
        l = ls if l is None else l + ls
    return acc / l


def _attn_prompt_kernel(q_ref, k_ref, v_ref, b_ref, o_ref, *, n_heads, dh, n_units):
    qi = pl.program_id(2)
    scale = dh ** -0.5

    def run(items):
        def raw_scores(item):
            q_rows, h, key_rows, _, _ = item
            hs = slice(h * dh, (h + 1) * dh)
            return _dot_nt(q_ref[q_rows, hs], k_ref[key_rows, hs])

        s_next = raw_scores(items[0])
        for n, item in enumerate(items):
            s_raw = s_next
            if n + 1 < len(items):
                s_next = raw_scores(items[n + 1])
            q_rows, h, key_rows, _, b_cols = item
            hs = slice(h * dh, (h + 1) * dh)
            o = _softmax_pv([s_raw * scale + b_ref[h, :, b_cols]], [v_ref[key_rows, hs]])
            o_ref[q_rows, hs] = o.astype(BF16)

    @pl.when(qi == 0)
    def _():
        items = []
        for u in range(n_units):
            nk = (u + 1) * UNIT
            for h in range(n_heads):
                items.append((slice(u * UNIT, (u + 1) * UNIT), h, slice(0, nk), nk, slice(WIN - nk, WIN)))
        run(items)

    @pl.when(qi > 0)
    def _():
        items = []
        for u in range(n_units):
            k0 = pl.multiple_of((qi * n_units + u) * UNIT - KV_TAIL, PAIR)
            for h in range(n_heads):
                items.append((slice(u * UNIT, (u + 1) * UNIT), h, pl.ds(k0, WIN), WIN, slice(0, WIN)))
        run(items)


def _attn_prompt(qkv, bias, n_batches, seq, d, dh, t_total):
    n_units = KV_TAIL // UNIT
    tq = n_units * UNIT
    assert seq % tq == 0 and seq >= WIN
    hw = min(d, 4 * dh)
    n_hg = d // hw
    n_heads = hw // dh
    return pl.pallas_call(
        functools.partial(_attn_prompt_kernel, n_heads=n_heads, dh=dh, n_units=n_units),
        grid=(n_batches, n_hg, seq // tq),
        in_specs=[pl.BlockSpec((tq, hw), lambda b, g, i: (b * (seq // tq) + i, g)),
                  pl.BlockSpec((seq, hw), lambda b, g, i: (b, n_hg + g)),
                  pl.BlockSpec((seq, hw), lambda b, g, i: (b, 2 * n_hg + g)),
                  pl.BlockSpec((n_heads, UNIT, WIN), lambda b, g, i: (g, 0, 0))],
        out_specs=pl.BlockSpec((tq, hw), lambda b, g, i: (b * (seq // tq) + i, g)),
        out_shape=jax.ShapeDtypeStruct((t_total, d), BF16),
        compiler_params=_params("arbitrary", "arbitrary", "arbitrary"),
        name="attn_prompt",
    )(qkv, qkv, qkv, bias)


def _attn_sample_kernel(q_ref, kn_ref, vn_ref, kc_ref, vc_ref, b_ref, oin_ref, o_ref, *, n_heads, dh):
    del oin_ref
    scale = dh ** -0.5
    ts = q_ref.shape[0]
    lc = kc_ref.shape[0] // n_heads

    def cached(ref, h):
        return ref[pl.ds(h, lc, stride=n_heads), :].astype(BF16)

    def raw_scores(h):
        hs = slice(h * dh, (h + 1) * dh)
        q = q_ref[:, hs]
        return _dot_nt(q, cached(kc_ref, h)), _dot_nt(q, kn_ref[:, hs])

    s_next = raw_scores(0)
    for h in range(n_heads):
        s_c, s_n = s_next
        if h + 1 < n_heads:
            s_next = raw_scores(h + 1)
        hs = slice(h * dh, (h + 1) * dh)
        o = _softmax_pv([s_c * scale + b_ref[h, :ts, :lc], s_n * scale + b_ref[h, :ts, lc:lc + ts]],
                        [cached(vc_ref, h), vn_ref[:, hs]])
        o_ref[:, hs] = o.astype(BF16)


def _attn_sample(qkv, k_cache, v_cache, layer, bias, o_all, row0, d, dh):
    _, n_b, lc, n_heads, _ = k_cache.shape
    ts = CHUNK
    assert lc == KV_TAIL and row0 % ts == 0 and n_heads * dh == d
    r0 = row0 // ts
    cache_shape = (k_cache.shape[0], n_b, lc * n_heads, dh)
    cache_spec = pl.BlockSpec((None, None, lc * n_heads, dh), lambda b: (layer, b, 0, 0))
    return pl.pallas_call(
        functools.partial(_attn_sample_kernel, n_heads=n_heads, dh=dh),
        grid=(n_b,),
        in_specs=[pl.BlockSpec((ts, d), lambda b: (r0 + b, 0)),
                  pl.BlockSpec((ts, d), lambda b: (r0 + b, 1)),
                  pl.BlockSpec((ts, d), lambda b: (r0 + b, 2)),
                  cache_spec, cache_spec,
                  pl.BlockSpec((n_heads, UNIT, WIN), lambda b: (0, 0, 0)),
                  pl.BlockSpec(memory_space=pl.ANY)],
        out_specs=pl.BlockSpec((ts, d), lambda b: (r0 + b, 0)),
        out_shape=jax.ShapeDtypeStruct(o_all.shape, o_all.dtype),
        input_output_aliases={6: 0},
        compiler_params=_params("arbitrary"),
        name="attn_sample",
    )(qkv, qkv, qkv, k_cache.reshape(cache_shape), v_cache.reshape(cache_shape), bias, o_all)


def _norm_blocked_kernel(x_ref, g_ref, o_ref, slab_scr):
    n_gb = o_ref.shape[0]
    tm = x_ref.shape[0]
    hn = _rms(x_ref[...], g_ref[...])
    for gb in range(n_gb):
        slab_scr[gb * tm:(gb + 1) * tm, :] = hn[:, gb * LANES:(gb + 1) * LANES]
    for gb in range(n_gb):
        for s in range(SSM_L):
            rows = pl.ds(gb * tm + s, tm // SSM_L, stride=SSM_L)
            o_ref[gb, :, s * LANES:(s + 1) * LANES] = slab_scr[rows, :].astype(BF16)


def _norm_blocked(x, gain):
    t, d = x.shape
    nc = t // SSM_L
    tm = _tile(t, 768, SSM_L * SUBLANES_BF16)
    n_gb = d // LANES
    return pl.pallas_call(
        _norm_blocked_kernel,
        grid=(t // tm,),
        in_specs=[pl.BlockSpec((tm, d), lambda i: (i, 0)),
                  pl.BlockSpec((1, d), lambda i: (0, 0))],
        out_specs=pl.BlockSpec((n_gb, tm // SSM_L, SSM_L * LANES), lambda i: (0, i, 0)),
        out_shape=jax.ShapeDtypeStruct((n_gb, nc, SSM_L * LANES), BF16),
        scratch_shapes=[pltpu.VMEM((n_gb * tm, LANES), F32)],
        compiler_params=_params("arbitrary"),
        name="ssm_norm",
    )(x, gain)


def _gelu_tanh(x):
    cdf = 0.5 * (1.0 + jnp.tanh(math.sqrt(2.0 / math.pi) * (x + 0.044715 * (x ** 3))))
    return x * cdf


def _ssm_core_kernel(u_ref, x0_ref, prow_ref, bxr_ref, bxi_ref, cxr_ref, cxi_ref, d_ref,
                     g_ref, xfp_ref, xfs_ref,
                     w_scr, cmt_scr, toep_scr, s_scr, xst_scr,
                     *, n_prompt, blocks_per_prompt, n_sample, blocks_per_sample, row_tile,
                     groups_per_tile):
    n_state = bxr_ref.shape[1]
    ch = bxr_ref.shape[0]

    def discretise(a_re, a_im, log_dt):
        dt = jnp.exp(log_dt)
        mag = jnp.exp(dt * a_re)
        l_re = mag * jnp.cos(dt * a_im)
        l_im = mag * jnp.sin(dt * a_im)
        den = a_re * a_re + a_im * a_im
        n_re = l_re - 1.0
        n_im = l_im
        z_re = (n_re * a_re + n_im * a_im) / den
        z_im = (n_im * a_re - n_re * a_im) / den
        return l_re, l_im, z_re, z_im

    def powers(l_re, l_im, n):
        out = [(jnp.ones_like(l_re), jnp.zeros_like(l_im))]
        for _ in range(n):
            pr, pi = out[-1]
            out.append((pr * l_re - pi * l_im, pr * l_im + pi * l_re))
        return out

    lr, li, zr, zi = discretise(prow_ref[0:1, :], prow_ref[1:2, :], prow_ref[2:3, :])
    prow = powers(lr, li, SSM_L)

    p = n_state // groups_per_tile
    sg = ch // groups_per_tile
    mb = (lax.broadcasted_iota(jnp.int32, (ch, n_state), 0) // sg ==
          lax.broadcasted_iota(jnp.int32, (ch, n_state), 1) // p)

    bxr = bxr_ref[...]
    bxi = bxi_ref[...]
    bb_r = jnp.where(mb, zr * bxr - zi * bxi, 0.0)
    bb_i = jnp.where(mb, zr * bxi + zi * bxr, 0.0)

    for s in range(SSM_L):
        pr, pi = prow[SSM_L - 1 - s]
        rows = slice(s * ch, (s + 1) * ch)
        w_scr[rows, 0:n_state] = (pr * bb_r - pi * bb_i).astype(BF16)
        w_scr[rows, n_state:2 * n_state] = (pr * bb_i + pi * bb_r).astype(BF16)

    cxr = jnp.where(mb, cxr_ref[...], 0.0)
    cxi = jnp.where(mb, cxi_ref[...], 0.0)
    for tau in range(SSM_L + 1):
        pr, pi = prow[tau]
        rows = slice(tau * ch, (tau + 1) * ch)
        cmt_scr[rows, 0:n_state] = (cxr * pr - cxi * pi).astype(BF16)
        cmt_scr[rows, n_state:2 * n_state] = (-(cxr * pi + cxi * pr)).astype(BF16)

    width = SSM_L * ch
    bbar = jnp.concatenate([bb_r.astype(BF16), bb_i.astype(BF16)], axis=1)
    k_all = _dot_nt(bbar, cmt_scr[0:width, :]).astype(BF16)
    for s in range(SSM_L):
        rows = slice(s * ch, (s + 1) * ch)
        if s % 2 == 1:
            toep_scr[rows, (s - 1) * ch:s * ch] = jnp.zeros((ch, ch), BF16)
        toep_scr[rows, s * ch:width] = k_all[:, 0:width - s * ch]

    n_rows = u_ref.shape[0]
    for r in range(n_rows // row_tile):
        rows = slice(r * row_tile, (r + 1) * row_tile)
        s_scr[rows, :] = _dot(u_ref[rows, :], w_scr[...])

    l16r, l16i = prow[SSM_L]

    def step(xr, xi, sr, si):
        return l16r * xr - l16i * xi + sr, l16r * xi + l16i * xr + si

    def prompt_body(c, carry):
        new = []
        for b in range(n_prompt):
            xr, xi = carry[b]
            row = b * blocks_per_prompt + c
            xst_scr[pl.ds(row, 1), 0:n_state] = xr
            xst_scr[pl.ds(row, 1), n_state:2 * n_state] = xi
            new.append(step(xr, xi, s_scr[pl.ds(row, 1), 0:n_state],
                            s_scr[pl.ds(row, 1), n_state:2 * n_state]))
        return tuple(new)

    zero = jnp.zeros((1, n_state), F32)
    fin = lax.fori_loop(0, blocks_per_prompt, prompt_body, tuple((zero, zero) for _ in range(n_prompt)))
    for b in range(n_prompt):
        xfp_ref[b:b + 1, 0:n_state] = fin[b][0]
        xfp_ref[b:b + 1, n_state:2 * n_state] = fin[b][1]

    base = n_prompt * blocks_per_prompt
    for b in range(n_sample):
        xr = x0_ref[b:b + 1, 0:n_state]
        xi = x0_ref[b:b + 1, n_state:2 * n_state]
        for j in range(blocks_per_sample):
            row = base + b * blocks_per_sample + j
            xst_scr[row:row + 1, 0:n_state] = xr
            xst_scr[row:row + 1, n_state:2 * n_state] = xi
            xr, xi = step(xr, xi, s_scr[row:row + 1, 0:n_state], s_scr[row:row + 1, n_state:2 * n_state])
        xfs_ref[b:b + 1, 0:n_state] = xr
        xfs_ref[b:b + 1, n_state:2 * n_state] = xi

    dvec = d_ref[...]
    for r in range(n_rows // row_tile):
        rows = slice(r * row_tile, (r + 1) * row_tile)
        xst = xst_scr[rows, :].astype(BF16)
        for t in range(0, SSM_L, 2):
            k_ext = (t + 2) * ch
            y = (_dot(u_ref[rows, 0:k_ext], toep_scr[0:k_ext, t * ch:(t + 2) * ch])
                 + _dot_nt(xst, cmt_scr[(t + 1) * ch:(t + 3) * ch, :]))
            for tt in (t, t + 1):
                cols = slice(tt * ch, (tt + 1) * ch)
                yt = y[:, (tt - t) * ch:(tt - t + 1) * ch] + dvec * u_ref[rows, cols].astype(F32)
                g_ref[rows, cols] = _gelu_tanh(yt).astype(BF16)


def _ssm_core(u_blk, x0, prow, bxr, bxi, cxr, cxi, dvec, n_prompt, blocks_per_prompt, n_sample,
              blocks_per_sample, groups_per_tile):
    n_gb, nc, width = u_blk.shape
    ch = LANES
    n_state = bxr.shape[2]
    assert width == SSM_L * ch and nc == n_prompt * blocks_per_prompt + n_sample * blocks_per_sample
    row_tile = _tile(nc, 384)
    kern = functools.partial(_ssm_core_kernel, n_prompt=n_prompt, blocks_per_prompt=blocks_per_prompt,
                             n_sample=n_sample, blocks_per_sample=blocks_per_sample, row_tile=row_tile,
                             groups_per_tile=groups_per_tile)
    return pl.pallas_call(
        kern,
        grid=(n_gb,),
        in_specs=[pl.BlockSpec((None, nc, width), lambda g: (g, 0, 0)),
                  pl.BlockSpec((n_sample, 2 * n_state), lambda g: (0, g)),
                  pl.BlockSpec((None, 3, n_state), lambda g: (g, 0, 0)),
                  pl.BlockSpec((None, ch, n_state), lambda g: (g, 0, 0)),
                  pl.BlockSpec((None, ch, n_state), lambda g: (g, 0, 0)),
                  pl.BlockSpec((None, ch, n_state), lambda g: (g, 0, 0)),
                  pl.BlockSpec((None, ch, n_state), lambda g: (g, 0, 0)),
                  pl.BlockSpec((1, ch), lambda g: (0, g))],
        out_specs=[pl.BlockSpec((None, nc, width), lambda g: (g, 0, 0)),
                   pl.BlockSpec((n_prompt, 2 * n_state), lambda g: (0, g)),
                   pl.BlockSpec((n_sample, 2 * n_state), lambda g: (0, g))],
        out_shape=[jax.ShapeDtypeStruct((n_gb, nc, width), BF16),
                   jax.ShapeDtypeStruct((n_prompt, n_gb * 2 * n_state), F32),
                   jax.ShapeDtypeStruct((n_sample, n_gb * 2 * n_state), F32)],
        scratch_shapes=[pltpu.VMEM((width, 2 * n_state), BF16),
                        pltpu.VMEM((width + ch, 2 * n_state), BF16),
                        pltpu.VMEM((width, width), BF16),
                        pltpu.VMEM((nc, 2 * n_state), F32),
                        pltpu.VMEM((nc, 2 * n_state), F32)],
        compiler_params=_params("arbitrary"),
        name="ssm_core",
    )(u_blk, x0, prow, bxr, bxi, cxr, cxi, dvec)


def _glu_res_kernel(g_ref, x_ref, wa_ref, wb_ref, o_ref, slab_scr, lhs_scr):
    n_gb = g_ref.shape[0]
    tm = x_ref.shape[0]

    for gb in range(n_gb):
        for s in range(SSM_L):
            rows = pl.ds(gb * tm + s, tm // SSM_L, stride=SSM_L)
            slab_scr[rows, :] = g_ref[gb, :, s * LANES:(s + 1) * LANES].astype(F32)
    for gb in range(n_gb):
        lhs_scr[:, gb * LANES:(gb + 1) * LANES] = slab_scr[gb * tm:(gb + 1) * tm, :].astype(BF16)

    g = lhs_scr[...]
    a = _dot(g, wa_ref[...])
    b = _dot(g, wb_ref[...])
    o_ref[...] = x_ref[...] + a * jax.nn.sigmoid(b)


def _glu_res(x, g_blk, w_a_all, w_b_all, layer):
    t, d = x.shape
    n_gb, nc, _ = g_blk.shape
    tm = _tile(t, 512, SSM_L * SUBLANES_BF16)
    w_spec = pl.BlockSpec((None, d, d), lambda i: (layer, 0, 0), pipeline_mode=pl.Buffered(1))
    return pl.pallas_call(
        _glu_res_kernel,
        grid=(t // tm,),
        in_specs=[pl.BlockSpec((n_gb, tm // SSM_L, SSM_L * LANES), lambda i: (0, i, 0)),
                  pl.BlockSpec((tm, d), lambda i: (i, 0)),
                  w_spec, w_spec],
        out_specs=pl.BlockSpec((tm, d), lambda i: (i, 0)),
        out_shape=jax.ShapeDtypeStruct((t, d), F32),
        scratch_shapes=[pltpu.VMEM((n_gb * tm, LANES), F32),
                        pltpu.VMEM((tm, d), BF16)],
        compiler_params=_params("arbitrary"),
        name="glu_res",
    )(g_blk, x, w_a_all, w_b_all)


def _ssm_layouts(a_re, a_im, log_dt, b_re, b_im, c_re, c_im):
    g, p = a_re.shape
    sg = b_re.shape[2]
    gl = LANES // sg
    n_gb = g // gl
    dt = jnp.broadcast_to(log_dt[:, None], (g, p))
    prow = jnp.stack([a_re, a_im, dt], axis=0).reshape(3, n_gb, gl * p).transpose(1, 0, 2)

    def expand_b(b):
        bt = b.reshape(n_gb, gl, p, sg).transpose(0, 3, 1, 2)
        bt = jnp.broadcast_to(bt[:, None], (n_gb, gl, sg, gl, p))
        return bt.reshape(n_gb, gl * sg, gl * p)

    def expand_c(c):
        ct = c.reshape(n_gb, gl, sg, 1, p)
        ct = jnp.broadcast_to(ct, (n_gb, gl, sg, gl, p))
        return ct.reshape(n_gb, gl * sg, gl * p)

    return prow, expand_b(b_re), expand_b(b_im), expand_c(c_re), expand_c(c_im)


def _state_to_tiles(st_re, st_im, gl):
    b, g, p = st_re.shape
    n_gb = g // gl
    st = jnp.stack([st_re.reshape(b, n_gb, gl * p), st_im.reshape(b, n_gb, gl * p)], axis=2)
    return st.reshape(b, n_gb * 2 * gl * p)


def _tiles_to_state(x, g, p, gl):
    b = x.shape[0]
    n_gb = g // gl
    x = x.reshape(b, n_gb, 2, gl, p)
    return x[:, :, 0].reshape(b, g, p), x[:, :, 1].reshape(b, g, p)


def kernel(x_prompt, x_sample, cache_attn_k, cache_attn_v, state_ssm_re, state_ssm_im, norm_mix, norm_mlp, norm_final, attn_w_qkv, attn_w_o, attn_rel_bias, ssm_a_re, ssm_a_im, ssm_log_dt, ssm_b_re, ssm_b_im, ssm_c_re, ssm_c_im, ssm_d, ssm_w_glu_a, ssm_w_glu_b, mlp_w_up, mlp_w_down):
    bp, seq, d = x_prompt.shape
    bs, ts, _ = x_sample.shape
    depth = norm_mix.shape[0]
    n_heads, dh = cache_attn_k.shape[3], cache_attn_k.shape[4]
    n_groups, p_state = ssm_a_re.shape[1], ssm_a_re.shape[2]
    sg = d // n_groups
    gl = LANES // sg
    assert ts == CHUNK and seq % KV_TAIL == 0 and (bs * ts) % KV_TAIL == 0
    assert seq % SSM_L == 0 and ts % SSM_L == 0 and d % LANES == 0

    n_prompt_rows = bp * seq
    t_total = n_prompt_rows + bs * ts
    xs = (x_prompt.reshape(n_prompt_rows, d), x_sample.reshape(bs * ts, d))

    w_qkv = attn_w_qkv.astype(BF16)
    w_o = attn_w_o.astype(BF16)
    w_glu_a = ssm_w_glu_a.astype(BF16)
    w_glu_b = ssm_w_glu_b.astype(BF16)
    w_up = mlp_w_up.astype(BF16)
    w_down = mlp_w_down.astype(BF16)

    k_p, v_p, k_s, v_s = [], [], [], []
    sr_p, si_p, sr_s, si_s = [], [], [], []
    for i in range(depth):
        j = i // 2
        gain = norm_mix[i][None, :]
        if i % 2 == 0:
            qkv, tail = _qkv(xs, gain, w_qkv, j, seq // KV_TAIL, bp)
            bias = _bias_table(attn_rel_bias[j])
            o = _attn_prompt(qkv, bias, bp, seq, d, dh, t_total)
            o = _attn_sample(qkv, cache_attn_k, cache_attn_v, j, bias, o, n_prompt_rows, d, dh)
            x = _proj_res(xs, o, w_o, j, n_prompt_rows)
            tail_p = tail[:bp * KV_TAIL].reshape(bp, KV_TAIL, 3, n_heads, dh)
            tail_s = tail[bp * KV_TAIL:].reshape(bs, ts, 3, n_heads, dh)
            k_p.append(tail_p[:, :, 1]); v_p.append(tail_p[:, :, 2])
            k_s.append(tail_s[:, :, 1]); v_s.append(tail_s[:, :, 2])
        else:
            x = xs[0]
            layouts = _ssm_layouts(ssm_a_re[j], ssm_a_im[j], ssm_log_dt[j], ssm_b_re[j], ssm_b_im[j],
                                   ssm_c_re[j], ssm_c_im[j])
            u_blk = _norm_blocked(x, gain)
            x0 = _state_to_tiles(state_ssm_re[j], state_ssm_im[j], gl)
            g_blk, xf_p, xf_s = _ssm_core(u_blk, x0, *layouts, ssm_d[j][None, :], bp, seq // SSM_L, bs,
                                          ts // SSM_L, gl)
            x = _glu_res(x, g_blk, w_glu_a, w_glu_b, j)
            rp, ip = _tiles_to_state(xf_p, n_groups, p_state, gl)
            rs, is_ = _tiles_to_state(xf_s, n_groups, p_state, gl)
            sr_p.append(rp); si_p.append(ip); sr_s.append(rs); si_s.append(is_)
        x = _mlp(x, norm_mlp[i][None, :], w_up, w_down, i)
        xs = (x,)

    y_p, y_s = _final_norm(xs[0], norm_final[None, :], n_prompt_rows)
    return (y_p.reshape(bp, seq, d), y_s.reshape(bs, ts, d),
            jnp.stack(k_p), jnp.stack(v_p), jnp.stack(sr_p), jnp.stack(si_p),
            jnp.stack(k_s), jnp.stack(v_s), jnp.stack(sr_s), jnp.stack(si_s))
```

```python
import functools
import math

import jax
import jax.numpy as jnp
from jax import lax
from jax.experimental import pallas as pl
from jax.experimental.pallas import tpu as pltpu

CHUNK = 64
N_PAST_CHUNKS = 8
RMS_EPS = 1e-5
NEG_INF = -1e30
LOG2_E = math.log2(math.e)

LANES = 128
SUBLANES_BF16 = 16
VMEM_LIMIT_BYTES = 58 * 1024 * 1024

PAIR = 2 * CHUNK
UNIT = 2 * PAIR
KV_TAIL = CHUNK * N_PAST_CHUNKS
WIN = KV_TAIL + UNIT
SSM_L = 16

F32 = jnp.float32
BF16 = jnp.bfloat16


def _params(*sem):
    return pltpu.CompilerParams(dimension_semantics=sem, vmem_limit_bytes=VMEM_LIMIT_BYTES)


def _tile(n, target, mult=SUBLANES_BF16):
    best = None
    for t in range(mult, min(n, target) + 1, mult):
        if n % t == 0:
            best = t
    assert best is not None, (n, target, mult)
    return best


def _rms(x, g):
    y = x * lax.rsqrt(jnp.mean(x * x, axis=-1, keepdims=True) + RMS_EPS)
    return y * g


def _dot(a, b):
    return jnp.dot(a, b, preferred_element_type=F32)


def _dot_nt(a, b):
    return lax.dot_general(a, b, (((1,), (1,)), ((), ())), preferred_element_type=F32)


def _x_specs(xs, tm, d, n_prompt_tiles, grid_rank, row_axis):
    def at(f):
        return lambda *idx: (f(idx[row_axis]), 0)
    if len(xs) == 1:
        return [pl.BlockSpec((tm, d), at(lambda i: i))]
    return [pl.BlockSpec((tm, d), at(lambda i: jnp.minimum(i, n_prompt_tiles - 1))),
            pl.BlockSpec((tm, d), at(lambda i: jnp.maximum(i - n_prompt_tiles, 0)))]


def _read_x(x_refs, i, n_prompt_tiles):
    if len(x_refs) == 1:
        return x_refs[0][...]
    return jnp.where(i < n_prompt_tiles, x_refs[0][...], x_refs[1][...])


def _qkv_kernel(*refs, n_src, n_prompt_tiles, q_scale):
    x_refs, (g_ref, w_ref, o_ref, tail_ref) = refs[:n_src], refs[n_src:]
    x = _read_x(x_refs, pl.program_id(1), n_prompt_tiles)
    hn = _rms(x, g_ref[...]).astype(BF16)
    acc = _dot(hn, w_ref[...])
    scale = jnp.where(pl.program_id(0) == 0, q_scale, 1.0).astype(F32)
    o_ref[...] = (acc * scale).astype(BF16)
    tail_ref[...] = acc


def _qkv(xs, gain, w_all, layer, n_prompt_tiles_per_batch, n_prompt_batches, q_scale):
    d = xs[0].shape[1]
    t = sum(x.shape[0] for x in xs)
    n = w_all.shape[2]
    tm = KV_TAIL
    tn = d
    nt = t // tm
    n_prompt_tiles = n_prompt_tiles_per_batch * n_prompt_batches
    n_slots = n_prompt_batches + (nt - n_prompt_tiles)

    def slot(i):
        return jnp.where(i < n_prompt_tiles, i // n_prompt_tiles_per_batch,
                         i - n_prompt_tiles + n_prompt_batches)

    return pl.pallas_call(
        functools.partial(_qkv_kernel, n_src=len(xs), n_prompt_tiles=n_prompt_tiles, q_scale=q_scale),
        grid=(n // tn, nt),
        in_specs=_x_specs(xs, tm, d, n_prompt_tiles, 2, 1) + [
            pl.BlockSpec((1, d), lambda j, i: (0, 0)),
            pl.BlockSpec((None, d, tn), lambda j, i: (layer, 0, j))],
        out_specs=[pl.BlockSpec((tm, tn), lambda j, i: (i, j)),
                   pl.BlockSpec((tm, tn), lambda j, i: (slot(i), j))],
        out_shape=[jax.ShapeDtypeStruct((t, n), BF16),
                   jax.ShapeDtypeStruct((n_slots * tm, n), F32)],
        compiler_params=_params("arbitrary", "arbitrary"),
        name="qkv",
    )(*xs, gain, w_all)


def _proj_res_kernel(*refs, n_src, n_prompt_tiles):
    x_refs, a_refs, (w_ref, o_ref) = refs[:n_src], refs[n_src:n_src + 2], refs[n_src + 2:]
    i = pl.program_id(0)
    o_ref[...] = _read_x(x_refs, i, n_prompt_tiles) + _dot(_read_x(a_refs, i, n_prompt_tiles), w_ref[...])


def _proj_res(xs, a_prompt, a_sample, w_all, layer):
    d = xs[0].shape[1]
    n_prompt_rows, k = a_prompt.shape
    t = n_prompt_rows + a_sample.shape[0]
    tm = _tile(math.gcd(n_prompt_rows, t - n_prompt_rows), 512)
    npt = n_prompt_rows // tm
    return pl.pallas_call(
        functools.partial(_proj_res_kernel, n_src=len(xs), n_prompt_tiles=npt),
        grid=(t // tm,),
        in_specs=_x_specs(xs, tm, d, npt, 1, 0) + _x_specs((a_prompt, a_sample), tm, k, npt, 1, 0) + [
            pl.BlockSpec((None, k, d), lambda i: (layer, 0, 0), pipeline_mode=pl.Buffered(1))],
        out_specs=pl.BlockSpec((tm, d), lambda i: (i, 0)),
        out_shape=jax.ShapeDtypeStruct((t, d), F32),
        compiler_params=_params("arbitrary"),
        name="proj_res",
    )(*xs, a_prompt, a_sample, w_all)


def _mlp_kernel(x_ref, g_ref, wu_ref, wd_ref, o_ref, hn_ref):
    k = pl.program_id(1)

    @pl.when(k == 0)
    def _():
        x = x_ref[...]
        hn_ref[...] = _rms(x, g_ref[...]).astype(BF16)
        o_ref[...] = x

    h = jnp.maximum(_dot(hn_ref[...], wu_ref[...]), 0.0)
    o_ref[...] += _dot((h * h).astype(BF16), wd_ref[...])


def _mlp(x, gain, w_up_all, w_down_all, layer):
    t, d = x.shape
    dff = w_up_all.shape[2]
    tm = _tile(t, 768)
    tf = _tile(dff, 1024, LANES)
    return pl.pallas_call(
        _mlp_kernel,
        grid=(t // tm, dff // tf),
        in_specs=[pl.BlockSpec((tm, d), lambda i, k: (i, 0)),
                  pl.BlockSpec((1, d), lambda i, k: (0, 0)),
                  pl.BlockSpec((None, d, tf), lambda i, k: (layer, 0, k)),
                  pl.BlockSpec((None, tf, d), lambda i, k: (layer, k, 0))],
        out_specs=pl.BlockSpec((tm, d), lambda i, k: (i, 0)),
        out_shape=jax.ShapeDtypeStruct((t, d), F32),
        scratch_shapes=[pltpu.VMEM((tm, d), BF16)],
        compiler_params=_params("arbitrary", "arbitrary"),
        name="mlp",
    )(x, gain, w_up_all, w_down_all)


def _final_norm_kernel(x_ref, g_ref, yp_ref, ys_ref, *, n_prompt_tiles):
    i = pl.program_id(0)
    y = _rms(x_ref[...], g_ref[...])

    @pl.when(i < n_prompt_tiles)
    def _():
        yp_ref[...] = y

    @pl.when(i >= n_prompt_tiles)
    def _():
        ys_ref[...] = y


def _final_norm(x, gain, n_prompt_rows):
    t, d = x.shape
    n_sample_rows = t - n_prompt_rows
    tm = _tile(math.gcd(n_prompt_rows, n_sample_rows), 512)
    npt = n_prompt_rows // tm
    return pl.pallas_call(
        functools.partial(_final_norm_kernel, n_prompt_tiles=npt),
        grid=(t // tm,),
        in_specs=[pl.BlockSpec((tm, d), lambda i: (i, 0)),
                  pl.BlockSpec((1, d), lambda i: (0, 0))],
        out_specs=[pl.BlockSpec((tm, d), lambda i: (jnp.minimum(i, npt - 1), 0)),
                   pl.BlockSpec((tm, d), lambda i: (jnp.maximum(i - npt, 0), 0))],
        out_shape=[jax.ShapeDtypeStruct((n_prompt_rows, d), F32),
                   jax.ShapeDtypeStruct((n_sample_rows, d), F32)],
        compiler_params=_params("arbitrary"),
        name="final_norm",
    )(x, gain)


def _bias_kernel(t_ref, o_ref):
    n_heads = o_ref.shape[0]
    width = t_ref.shape[1]
    r = lax.broadcasted_iota(jnp.int32, (UNIT, WIN), 0)
    c = lax.broadcasted_iota(jnp.int32, (UNIT, WIN), 1)
    dist = r // PAIR + KV_TAIL // PAIR - c // PAIR
    q_hi = (r % PAIR) >= CHUNK
    k_hi = (c % PAIR) >= CHUNK
    masked = ((dist < 0) | (dist > KV_TAIL // PAIR)
              | ((dist == 0) & jnp.logical_not(q_hi) & k_hi)
              | ((dist == KV_TAIL // PAIR) & q_hi & jnp.logical_not(k_hi)))
    for h in range(n_heads):
        rows = jnp.broadcast_to(t_ref[h:h + 1, :], (UNIT, width))
        rolled = pltpu.roll(rows, width - (UNIT - 1), 1, stride=1, stride_axis=0)
        o_ref[h] = jnp.where(masked, NEG_INF, rolled[:, :WIN] * LOG2_E)


def _bias_table(rel_bias):
    n_heads, n_rel = rel_bias.shape
    max_rel = (n_rel - 1) // 2
    assert max_rel == PAIR
    width = WIN + UNIT
    n_far = WIN - 1 - max_rel + 1
    tvec = jnp.concatenate(
        [jnp.broadcast_to(rel_bias[:, 2 * max_rel:], (n_heads, n_far)),
         jnp.flip(rel_bias[:, :2 * max_rel], axis=1),
         jnp.broadcast_to(rel_bias[:, :1], (n_heads, width - n_far - 2 * max_rel))], axis=1)
    return pl.pallas_call(
        _bias_kernel,
        out_shape=jax.ShapeDtypeStruct((n_heads, UNIT, WIN), F32),
        compiler_params=pltpu.CompilerParams(vmem_limit_bytes=VMEM_LIMIT_BYTES),
        name="attn_bias",
    )(tvec)


def _softmax_pv(scores, values):
    dh = values[0].shape[1]
    m = scores[0].max(axis=1, keepdims=True)
    for s in scores[1:]:
        m = jnp.maximum(m, s.max(axis=1, keepdims=True))
    acc = None
    for s, v in zip(scores, values):
        p = jnp.exp2(s - m).astype(BF16)
        r = _dot(p, jnp.concatenate([v, jnp.ones_like(v)], axis=1))
        acc = r if acc is None else acc + r
    return acc[:, :dh] / acc[:, dh:]


def _attn_prompt_kernel(q_ref, k_ref, v_ref, b_ref, o_ref, *, n_heads, dh, n_units):
    qi = pl.program_id(2)

    def run(items):
        def raw_scores(item):
            q_rows, h, key_rows, _, _ = item
            hs = slice(h * dh, (h + 1) * dh)
            return _dot_nt(q_ref[q_rows, hs], k_ref[key_rows, hs])

        s_next = raw_scores(items[0])
        for n, item in enumerate(items):
            s_raw = s_next
            if n + 1 < len(items):
                s_next = raw_scores(items[n + 1])
            q_rows, h, key_rows, _, b_cols = item
            hs = slice(h * dh, (h + 1) * dh)
            o = _softmax_pv([s_raw + b_ref[h, :, b_cols]], [v_ref[key_rows, hs]])
            o_ref[q_rows, hs] = o.astype(BF16)

    @pl.when(qi == 0)
    def _():
        items = []
        for u in range(n_units):
            nk = (u + 1) * UNIT
            for h in range(n_heads):
                items.append((slice(u * UNIT, (u + 1) * UNIT), h, slice(0, nk), nk, slice(WIN - nk, WIN)))
        run(items)

    @pl.when(qi > 0)
    def _():
        items = []
        for u in range(n_units):
            k0 = pl.multiple_of((qi * n_units + u) * UNIT - KV_TAIL, PAIR)
            for h in range(n_heads):
                items.append((slice(u * UNIT, (u + 1) * UNIT), h, pl.ds(k0, WIN), WIN, slice(0, WIN)))
        run(items)


def _attn_prompt(qkv, bias, n_batches, seq, d, dh):
    n_units = KV_TAIL // UNIT
    tq = n_units * UNIT
    assert seq % tq == 0 and seq >= WIN
    hw = min(d, 4 * dh)
    n_hg = d // hw
    n_heads = hw // dh
    return pl.pallas_call(
        functools.partial(_attn_prompt_kernel, n_heads=n_heads, dh=dh, n_units=n_units),
        grid=(n_batches, n_hg, seq // tq),
        in_specs=[pl.BlockSpec((tq, hw), lambda b, g, i: (b * (seq // tq) + i, g)),
                  pl.BlockSpec((seq, hw), lambda b, g, i: (b, n_hg + g)),
                  pl.BlockSpec((seq, hw), lambda b, g, i: (b, 2 * n_hg + g)),
                  pl.BlockSpec((n_heads, UNIT, WIN), lambda b, g, i: (g, 0, 0))],
        out_specs=pl.BlockSpec((tq, hw), lambda b, g, i: (b * (seq // tq) + i, g)),
        out_shape=jax.ShapeDtypeStruct((n_batches * seq, d), BF16),
        compiler_params=_params("arbitrary", "arbitrary", "arbitrary"),
        name="attn_prompt",
    )(qkv, qkv, qkv, bias)


def _attn_sample_kernel(q_ref, kn_ref, vn_ref, kc_ref, vc_ref, b_ref, o_ref, *, n_heads, dh):
    ts = q_ref.shape[0]
    lc = kc_ref.shape[0] // n_heads

    def cached(ref, h):
        return ref[pl.ds(h, lc, stride=n_heads), :].astype(BF16)

    def raw_scores(h):
        hs = slice(h * dh, (h + 1) * dh)
        q = q_ref[:, hs]
        return _dot_nt(q, cached(kc_ref, h)), _dot_nt(q, kn_ref[:, hs])

    s_next = raw_scores(0)
    for h in range(n_heads):
        s_c, s_n = s_next
        if h + 1 < n_heads:
            s_next = raw_scores(h + 1)
        hs = slice(h * dh, (h + 1) * dh)
        o = _softmax_pv([s_c + b_ref[h, :ts, :lc], s_n + b_ref[h, :ts, lc:lc + ts]],
                        [cached(vc_ref, h), vn_ref[:, hs]])
        o_ref[:, hs] = o.astype(BF16)


def _attn_sample(qkv, k_cache, v_cache, layer, bias, row0, d, dh):
    _, n_b, lc, n_heads, _ = k_cache.shape
    ts = CHUNK
    assert lc == KV_TAIL and row0 % ts == 0 and n_heads * dh == d
    r0 = row0 // ts
    cache_shape = (k_cache.shape[0], n_b, lc * n_heads, dh)
    cache_spec = pl.BlockSpec((None, None, lc * n_heads, dh), lambda b: (layer, b, 0, 0))
    return pl.pallas_call(
        functools.partial(_attn_sample_kernel, n_heads=n_heads, dh=dh),
        grid=(n_b,),
        in_specs=[pl.BlockSpec((ts, d), lambda b: (r0 + b, 0)),
                  pl.BlockSpec((ts, d), lambda b: (r0 + b, 1)),
                  pl.BlockSpec((ts, d), lambda b: (r0 + b, 2)),
                  cache_spec, cache_spec,
                  pl.BlockSpec((n_heads, UNIT, WIN), lambda b: (0, 0, 0))],
        out_specs=pl.BlockSpec((ts, d), lambda b: (b, 0)),
        out_shape=jax.ShapeDtypeStruct((n_b * ts, d), BF16),
        compiler_params=_params("arbitrary"),
        name="attn_sample",
    )(qkv, qkv, qkv, k_cache.reshape(cache_shape), v_cache.reshape(cache_shape), bias)


def _norm_blocked_kernel(x_ref, g_ref, o_ref, slab_scr):
    n_gb = o_ref.shape[0]
    tm = x_ref.shape[0]
    hn = _rms(x_ref[...], g_ref[...])
    for gb in range(n_gb):
        slab_scr[gb * tm:(gb + 1) * tm, :] = hn[:, gb * LANES:(gb + 1) * LANES]
    for gb in range(n_gb):
        for s in range(SSM_L):
            rows = pl.ds(gb * tm + s, tm // SSM_L, stride=SSM_L)
            o_ref[gb, :, s * LANES:(s + 1) * LANES] = slab_scr[rows, :].astype(BF16)


def _norm_blocked(x, gain):
    t, d = x.shape
    nc = t // SSM_L
    tm = _tile(t, 768, SSM_L * SUBLANES_BF16)
    n_gb = d // LANES
    return pl.pallas_call(
        _norm_blocked_kernel,
        grid=(t // tm,),
        in_specs=[pl.BlockSpec((tm, d), lambda i: (i, 0)),
                  pl.BlockSpec((1, d), lambda i: (0, 0))],
        out_specs=pl.BlockSpec((n_gb, tm // SSM_L, SSM_L * LANES), lambda i: (0, i, 0)),
        out_shape=jax.ShapeDtypeStruct((n_gb, nc, SSM_L * LANES), BF16),
        scratch_shapes=[pltpu.VMEM((n_gb * tm, LANES), F32)],
        compiler_params=_params("arbitrary"),
        name="ssm_norm",
    )(x, gain)


def _gelu_tanh(x):
    cdf = 0.5 * (1.0 + jnp.tanh(math.sqrt(2.0 / math.pi) * (x + 0.044715 * (x ** 3))))
    return x * cdf


def _ssm_core_kernel(u_ref, x0_ref, prow_ref, bxr_ref, bxi_ref, cxr_ref, cxi_ref, d_ref,
                     g_ref, xfp_ref, xfs_ref,
                     w_scr, cmt_scr, toep_scr, s_scr, xst_scr,
                     *, n_prompt, blocks_per_prompt, n_sample, blocks_per_sample, row_tile,
                     groups_per_tile):
    n_state = bxr_ref.shape[1]
    ch = bxr_ref.shape[0]

    def discretise(a_re, a_im, log_dt):
        dt = jnp.exp(log_dt)
        mag = jnp.exp(dt * a_re)
        l_re = mag * jnp.cos(dt * a_im)
        l_im = mag * jnp.sin(dt * a_im)
        den = a_re * a_re + a_im * a_im
        n_re = l_re - 1.0
        n_im = l_im
        z_re = (n_re * a_re + n_im * a_im) / den
        z_im = (n_im * a_re - n_re * a_im) / den
        return l_re, l_im, z_re, z_im

    def powers(l_re, l_im, n):
        out = [(jnp.ones_like(l_re), jnp.zeros_like(l_im))]
        for _ in range(n):
            pr, pi = out[-1]
            out.append((pr * l_re - pi * l_im, pr * l_im + pi * l_re))
        return out

    lr, li, zr, zi = discretise(prow_ref[0:1, :], prow_ref[1:2, :], prow_ref[2:3, :])
    prow = powers(lr, li, SSM_L)

    p = n_state // groups_per_tile
    sg = ch // groups_per_tile
    mb = (lax.broadcasted_iota(jnp.int32, (ch, n_state), 0) // sg ==
          lax.broadcasted_iota(jnp.int32, (ch, n_state), 1) // p)

    bxr = bxr_ref[...]
    bxi = bxi_ref[...]
    bb_r = jnp.where(mb, zr * bxr - zi * bxi, 0.0)
    bb_i = jnp.where(mb, zr * bxi + zi * bxr, 0.0)

    for s in range(SSM_L):
        pr, pi = prow[SSM_L - 1 - s]
        rows = slice(s * ch, (s + 1) * ch)
        w_scr[rows, 0:n_state] = (pr * bb_r - pi * bb_i).astype(BF16)
        w_scr[rows, n_state:2 * n_state] = (pr * bb_i + pi * bb_r).astype(BF16)

    cxr = jnp.where(mb, cxr_ref[...], 0.0)
    cxi = jnp.where(mb, cxi_ref[...], 0.0)
    for tau in range(SSM_L + 1):
        pr, pi = prow[tau]
        rows = slice(tau * ch, (tau + 1) * ch)
        cmt_scr[rows, 0:n_state] = (cxr * pr - cxi * pi).astype(BF16)
        cmt_scr[rows, n_state:2 * n_state] = (-(cxr * pi + cxi * pr)).astype(BF16)

    width = SSM_L * ch
    bbar = jnp.concatenate([bb_r.astype(BF16), bb_i.astype(BF16)], axis=1)
    k_all = _dot_nt(bbar, cmt_scr[0:width, :]).astype(BF16)
    for s in range(SSM_L):
        rows = slice(s * ch, (s + 1) * ch)
        if s % 2 == 1:
            toep_scr[rows, (s - 1) * ch:s * ch] = jnp.zeros((ch, ch), BF16)
        toep_scr[rows, s * ch:width] = k_all[:, 0:width - s * ch]

    n_rows = u_ref.shape[0]
    for r in range(n_rows // row_tile):
        rows = slice(r * row_tile, (r + 1) * row_tile)
        s_scr[rows, :] = _dot(u_ref[rows, :], w_scr[...])

    l16r, l16i = prow[SSM_L]

    def step(xr, xi, sr, si):
        return l16r * xr - l16i * xi + sr, l16r * xi + l16i * xr + si

    def prompt_body(c, carry):
        new = []
        for b in range(n_prompt):
            xr, xi = carry[b]
            row = b * blocks_per_prompt + c
            xst_scr[pl.ds(row, 1), 0:n_state] = xr
            xst_scr[pl.ds(row, 1), n_state:2 * n_state] = xi
            new.append(step(xr, xi, s_scr[pl.ds(row, 1), 0:n_state],
                            s_scr[pl.ds(row, 1), n_state:2 * n_state]))
        return tuple(new)

    zero = jnp.zeros((1, n_state), F32)
    fin = lax.fori_loop(0, blocks_per_prompt, prompt_body, tuple((zero, zero) for _ in range(n_prompt)))
    for b in range(n_prompt):
        xfp_ref[b:b + 1, 0:n_state] = fin[b][0]
        xfp_ref[b:b + 1, n_state:2 * n_state] = fin[b][1]

    base = n_prompt * blocks_per_prompt
    for b in range(n_sample):
        xr = x0_ref[b:b + 1, 0:n_state]
        xi = x0_ref[b:b + 1, n_state:2 * n_state]
        for j in range(blocks_per_sample):
            row = base + b * blocks_per_sample + j
            xst_scr[row:row + 1, 0:n_state] = xr
            xst_scr[row:row + 1, n_state:2 * n_state] = xi
            xr, xi = step(xr, xi, s_scr[row:row + 1, 0:n_state], s_scr[row:row + 1, n_state:2 * n_state])
        xfs_ref[b:b + 1, 0:n_state] = xr
        xfs_ref[b:b + 1, n_state:2 * n_state] = xi

    dvec = d_ref[...]
    for r in range(n_rows // row_tile):
        rows = slice(r * row_tile, (r + 1) * row_tile)
        xst = xst_scr[rows, :].astype(BF16)
        for t in range(0, SSM_L, 2):
            k_ext = (t + 2) * ch
            y = (_dot(u_ref[rows, 0:k_ext], toep_scr[0:k_ext, t * ch:(t + 2) * ch])
                 + _dot_nt(xst, cmt_scr[(t + 1) * ch:(t + 3) * ch, :]))
            for tt in (t, t + 1):
                cols = slice(tt * ch, (tt + 1) * ch)
                yt = y[:, (tt - t) * ch:(tt - t + 1) * ch] + dvec * u_ref[rows, cols].astype(F32)
                g_ref[rows, cols] = _gelu_tanh(yt).astype(BF16)


def _ssm_core(u_blk, x0, prow, bxr, bxi, cxr, cxi, dvec, n_prompt, blocks_per_prompt, n_sample,
              blocks_per_sample, groups_per_tile):
    n_gb, nc, width = u_blk.shape
    ch = LANES
    n_state = bxr.shape[2]
    assert width == SSM_L * ch and nc == n_prompt * blocks_per_prompt + n_sample * blocks_per_sample
    row_tile = _tile(nc, 384)
    kern = functools.partial(_ssm_core_kernel, n_prompt=n_prompt, blocks_per_prompt=blocks_per_prompt,
                             n_sample=n_sample, blocks_per_sample=blocks_per_sample, row_tile=row_tile,
                             groups_per_tile=groups_per_tile)
    return pl.pallas_call(
        kern,
        grid=(n_gb,),
        in_specs=[pl.BlockSpec((None, nc, width), lambda g: (g, 0, 0)),
                  pl.BlockSpec((n_sample, 2 * n_state), lambda g: (0, g)),
                  pl.BlockSpec((None, 3, n_state), lambda g: (g, 0, 0)),
                  pl.BlockSpec((None, ch, n_state), lambda g: (g, 0, 0)),
                  pl.BlockSpec((None, ch, n_state), lambda g: (g, 0, 0)),
                  pl.BlockSpec((None, ch, n_state), lambda g: (g, 0, 0)),
                  pl.BlockSpec((None, ch, n_state), lambda g: (g, 0, 0)),
                  pl.BlockSpec((1, ch), lambda g: (0, g))],
        out_specs=[pl.BlockSpec((None, nc, width), lambda g: (g, 0, 0)),
                   pl.BlockSpec((n_prompt, 2 * n_state), lambda g: (0, g)),
                   pl.BlockSpec((n_sample, 2 * n_state), lambda g: (0, g))],
        out_shape=[jax.ShapeDtypeStruct((n_gb, nc, width), BF16),
                   jax.ShapeDtypeStruct((n_prompt, n_gb * 2 * n_state), F32),
                   jax.ShapeDtypeStruct((n_sample, n_gb * 2 * n_state), F32)],
        scratch_shapes=[pltpu.VMEM((width, 2 * n_state), BF16),
                        pltpu.VMEM((width + ch, 2 * n_state), BF16),
                        pltpu.VMEM((width, width), BF16),
                        pltpu.VMEM((nc, 2 * n_state), F32),
                        pltpu.VMEM((nc, 2 * n_state), F32)],
        compiler_params=_params("arbitrary"),
        name="ssm_core",
    )(u_blk, x0, prow, bxr, bxi, cxr, cxi, dvec)


def _glu_res_kernel(g_ref, x_ref, wa_ref, wb_ref, o_ref, slab_scr, lhs_scr):
    n_gb = g_ref.shape[0]
    tm = x_ref.shape[0]

    for gb in range(n_gb):
        for s in range(SSM_L):
            rows = pl.ds(gb * tm + s, tm // SSM_L, stride=SSM_L)
            slab_scr[rows, :] = g_ref[gb, :, s * LANES:(s + 1) * LANES].astype(F32)
    for gb in range(n_gb):
        lhs_scr[:, gb * LANES:(gb + 1) * LANES] = slab_scr[gb * tm:(gb + 1) * tm, :].astype(BF16)

    g = lhs_scr[...]
    a = _dot(g, wa_ref[...])
    b = _dot(g, wb_ref[...])
    o_ref[...] = x_ref[...] + a * jax.nn.sigmoid(b)


def _glu_res(x, g_blk, w_a_all, w_b_all, layer):
    t, d = x.shape
    n_gb, nc, _ = g_blk.shape
    tm = _tile(t, 512, SSM_L * SUBLANES_BF16)
    w_spec = pl.BlockSpec((None, d, d), lambda i: (layer, 0, 0), pipeline_mode=pl.Buffered(1))
    return pl.pallas_call(
        _glu_res_kernel,
        grid=(t // tm,),
        in_specs=[pl.BlockSpec((n_gb, tm // SSM_L, SSM_L * LANES), lambda i: (0, i, 0)),
                  pl.BlockSpec((tm, d), lambda i: (i, 0)),
                  w_spec, w_spec],
        out_specs=pl.BlockSpec((tm, d), lambda i: (i, 0)),
        out_shape=jax.ShapeDtypeStruct((t, d), F32),
        scratch_shapes=[pltpu.VMEM((n_gb * tm, LANES), F32),
                        pltpu.VMEM((tm, d), BF16)],
        compiler_params=_params("arbitrary"),
        name="glu_res",
    )(g_blk, x, w_a_all, w_b_all)


def _ssm_layouts(a_re, a_im, log_dt, b_re, b_im, c_re, c_im):
    g, p = a_re.shape
    sg = b_re.shape[2]
    gl = LANES // sg
    n_gb = g // gl
    dt = jnp.broadcast_to(log_dt[:, None], (g, p))
    prow = jnp.stack([a_re, a_im, dt], axis=0).reshape(3, n_gb, gl * p).transpose(1, 0, 2)

    def expand_b(b):
        bt = b.reshape(n_gb, gl, p, sg).transpose(0, 3, 1, 2)
        bt = jnp.broadcast_to(bt[:, None], (n_gb, gl, sg, gl, p))
        return bt.reshape(n_gb, gl * sg, gl * p)

    def expand_c(c):
        ct = c.reshape(n_gb, gl, sg, 1, p)
        ct = jnp.broadcast_to(ct, (n_gb, gl, sg, gl, p))
        return ct.reshape(n_gb, gl * sg, gl * p)

    return prow, expand_b(b_re), expand_b(b_im), expand_c(c_re), expand_c(c_im)


def _state_to_tiles(st_re, st_im, gl):
    b, g, p = st_re.shape
    n_gb = g // gl
    st = jnp.stack([st_re.reshape(b, n_gb, gl * p), st_im.reshape(b, n_gb, gl * p)], axis=2)
    return st.reshape(b, n_gb * 2 * gl * p)


def _tiles_to_state(x, g, p, gl):
    b = x.shape[0]
    n_gb = g // gl
    x = x.reshape(b, n_gb, 2, gl, p)
    return x[:, :, 0].reshape(b, g, p), x[:, :, 1].reshape(b, g, p)


def kernel(x_prompt, x_sample, cache_attn_k, cache_attn_v, state_ssm_re, state_ssm_im, norm_mix, norm_mlp, norm_final, attn_w_qkv, attn_w_o, attn_rel_bias, ssm_a_re, ssm_a_im, ssm_log_dt, ssm_b_re, ssm_b_im, ssm_c_re, ssm_c_im, ssm_d, ssm_w_glu_a, ssm_w_glu_b, mlp_w_up, mlp_w_down):
    bp, seq, d = x_prompt.shape
    bs, ts, _ = x_sample.shape
    depth = norm_mix.shape[0]
    n_heads, dh = cache_attn_k.shape[3], cache_attn_k.shape[4]
    n_groups, p_state = ssm_a_re.shape[1], ssm_a_re.shape[2]
    sg = d // n_groups
    gl = LANES // sg
    assert ts == CHUNK and seq % KV_TAIL == 0 and (bs * ts) % KV_TAIL == 0
    assert seq % SSM_L == 0 and ts % SSM_L == 0 and d % LANES == 0

    n_prompt_rows = bp * seq
    xs = (x_prompt.reshape(n_prompt_rows, d), x_sample.reshape(bs * ts, d))

    w_qkv = attn_w_qkv.astype(BF16)
    w_o = attn_w_o.astype(BF16)
    w_glu_a = ssm_w_glu_a.astype(BF16)
    w_glu_b = ssm_w_glu_b.astype(BF16)
    w_up = mlp_w_up.astype(BF16)
    w_down = mlp_w_down.astype(BF16)

    k_p, v_p, k_s, v_s = [], [], [], []
    sr_p, si_p, sr_s, si_s = [], [], [], []
    for i in range(depth):
        j = i // 2
        gain = norm_mix[i][None, :]
        if i % 2 == 0:
            qkv, tail = _qkv(xs, gain, w_qkv, j, seq // KV_TAIL, bp, dh ** -0.5 * LOG2_E)
            bias = _bias_table(attn_rel_bias[j])
            o_p = _attn_prompt(qkv, bias, bp, seq, d, dh)
            o_s = _attn_sample(qkv, cache_attn_k, cache_attn_v, j, bias, n_prompt_rows, d, dh)
            x = _proj_res(xs, o_p, o_s, w_o, j)
            tail_p = tail[:bp * KV_TAIL].reshape(bp, KV_TAIL, 3, n_heads, dh)
            tail_s = tail[bp * KV_TAIL:].reshape(bs, ts, 3, n_heads, dh)
            k_p.append(tail_p[:, :, 1]); v_p.append(tail_p[:, :, 2])
            k_s.append(tail_s[:, :, 1]); v_s.append(tail_s[:, :, 2])
        else:
            x = xs[0]
            layouts = _ssm_layouts(ssm_a_re[j], ssm_a_im[j], ssm_log_dt[j], ssm_b_re[j], ssm_b_im[j],
                                   ssm_c_re[j], ssm_c_im[j])
            u_blk = _norm_blocked(x, gain)
            x0 = _state_to_tiles(state_ssm_re[j], state_ssm_im[j], gl)
            g_blk, xf_p, xf_s = _ssm_core(u_blk, x0, *layouts, ssm_d[j][None, :], bp, seq // SSM_L, bs,
                                          ts // SSM_L, gl)
            x = _glu_res(x, g_blk, w_glu_a, w_glu_b, j)
            rp, ip = _tiles_to_state(xf_p, n_groups, p_state, gl)
            rs, is_ = _tiles_to_state(xf_s, n_groups, p_state, gl)
            sr_p.append(rp); si_p.append(ip); sr_s.append(rs); si_s.append(is_)
        x = _mlp(x, norm_mlp[i][None, :], w_up, w_down, i)
        xs = (x,)

    y_p, y_s = _final_norm(xs[0], norm_final[None, :], n_prompt_rows)
    return (y_p.reshape(bp, seq, d), y_s.reshape(bs, ts, d),
            jnp.stack(k_p), jnp.stack(v_p), jnp.stack(sr_p), jnp.stack(si_p),
            jnp.stack(k_s), jnp.stack(v_s), jnp.stack(sr_s), jnp.stack(si_s))
```

```python
import functools
import math

import jax
import jax.numpy as jnp
from jax import lax
from jax.experimental import pallas as pl
from jax.experimental.pallas import tpu as pltpu

CHUNK = 64
N_PAST_CHUNKS = 8
RMS_EPS = 1e-5
NEG_INF = -1e30
LOG2_E = math.log2(math.e)

LANES = 128
SUBLANES_BF16 = 16
VMEM_LIMIT_BYTES = 58 * 1024 * 1024

PAIR = 2 * CHUNK
UNIT = 2 * PAIR
KV_TAIL = CHUNK * N_PAST_CHUNKS
WIN = KV_TAIL + UNIT
SSM_L = 16

F32 = jnp.float32
BF16 = jnp.bfloat16


def _params(*sem):
    return pltpu.CompilerParams(dimension_semantics=sem, vmem_limit_bytes=VMEM_LIMIT_BYTES)


def _tile(n, target, mult=SUBLANES_BF16):
    best = None
    for t in range(mult, min(n, target) + 1, mult):
        if n % t == 0:
            best = t
    assert best is not None, (n, target, mult)
    return best


def _rms(x, g):
    y = x * lax.rsqrt(jnp.mean(x * x, axis=-1, keepdims=True) + RMS_EPS)
    return y * g


def _dot(a, b):
    return jnp.dot(a, b, preferred_element_type=F32)


def _dot_nt(a, b):
    return lax.dot_general(a, b, (((1,), (1,)), ((), ())), preferred_element_type=F32)


def _x_specs(xs, tm, d, n_prompt_tiles, grid_rank, row_axis):
    def at(f):
        return lambda *idx: (f(idx[row_axis]), 0)
    if len(xs) == 1:
        return [pl.BlockSpec((tm, d), at(lambda i: i))]
    return [pl.BlockSpec((tm, d), at(lambda i: jnp.minimum(i, n_prompt_tiles - 1))),
            pl.BlockSpec((tm, d), at(lambda i: jnp.maximum(i - n_prompt_tiles, 0)))]


def _read_x(x_refs, i, n_prompt_tiles):
    if len(x_refs) == 1:
        return x_refs[0][...]
    return jnp.where(i < n_prompt_tiles, x_refs[0][...], x_refs[1][...])


def _qkv_kernel(*refs, n_src, n_prompt_tiles, q_scale):
    x_refs, (g_ref, w_ref, o_ref, tail_ref) = refs[:n_src], refs[n_src:]
    x = _read_x(x_refs, pl.program_id(1), n_prompt_tiles)
    hn = _rms(x, g_ref[...]).astype(BF16)
    acc = _dot(hn, w_ref[...])
    scale = jnp.where(pl.program_id(0) == 0, q_scale, 1.0).astype(F32)
    o_ref[...] = (acc * scale).astype(BF16)
    tail_ref[...] = acc


def _qkv(xs, gain, w_all, layer, n_prompt_tiles_per_batch, n_prompt_batches, q_scale):
    d = xs[0].shape[1]
    t = sum(x.shape[0] for x in xs)
    n = w_all.shape[2]
    tm = KV_TAIL
    tn = d
    nt = t // tm
    n_prompt_tiles = n_prompt_tiles_per_batch * n_prompt_batches
    n_slots = n_prompt_batches + (nt - n_prompt_tiles)

    def slot(i):
        return jnp.where(i < n_prompt_tiles, i // n_prompt_tiles_per_batch,
                         i - n_prompt_tiles + n_prompt_batches)

    return pl.pallas_call(
        functools.partial(_qkv_kernel, n_src=len(xs), n_prompt_tiles=n_prompt_tiles, q_scale=q_scale),
        grid=(n // tn, nt),
        in_specs=_x_specs(xs, tm, d, n_prompt_tiles, 2, 1) + [
            pl.BlockSpec((1, d), lambda j, i: (0, 0)),
            pl.BlockSpec((None, d, tn), lambda j, i: (layer, 0, j))],
        out_specs=[pl.BlockSpec((tm, tn), lambda j, i: (i, j)),
                   pl.BlockSpec((tm, tn), lambda j, i: (slot(i), j))],
        out_shape=[jax.ShapeDtypeStruct((t, n), BF16),
                   jax.ShapeDtypeStruct((n_slots * tm, n), F32)],
        compiler_params=_params("arbitrary", "arbitrary"),
        name="qkv",
    )(*xs, gain, w_all)


def _proj_res_kernel(*refs, n_src, n_prompt_tiles):
    x_refs, a_refs, (w_ref, o_ref) = refs[:n_src], refs[n_src:n_src + 2], refs[n_src + 2:]
    i = pl.program_id(0)
    o_ref[...] = _read_x(x_refs, i, n_prompt_tiles) + _dot(_read_x(a_refs, i, n_prompt_tiles), w_ref[...])


def _proj_res(xs, a_prompt, a_sample, w_all, layer):
    d = xs[0].shape[1]
    n_prompt_rows, k = a_prompt.shape
    t = n_prompt_rows + a_sample.shape[0]
    tm = _tile(math.gcd(n_prompt_rows, t - n_prompt_rows), 512)
    npt = n_prompt_rows // tm
    return pl.pallas_call(
        functools.partial(_proj_res_kernel, n_src=len(xs), n_prompt_tiles=npt),
        grid=(t // tm,),
        in_specs=_x_specs(xs, tm, d, npt, 1, 0) + _x_specs((a_prompt, a_sample), tm, k, npt, 1, 0) + [
            pl.BlockSpec((None, k, d), lambda i: (layer, 0, 0), pipeline_mode=pl.Buffered(1))],
        out_specs=pl.BlockSpec((tm, d), lambda i: (i, 0)),
        out_shape=jax.ShapeDtypeStruct((t, d), F32),
        compiler_params=_params("arbitrary"),
        name="proj_res",
    )(*xs, a_prompt, a_sample, w_all)


def _mlp_kernel(x_ref, g_ref, wu_ref, wd_ref, o_ref, hn_ref):
    k = pl.program_id(1)

    @pl.when(k == 0)
    def _():
        x = x_ref[...]
        hn_ref[...] = _rms(x, g_ref[...]).astype(BF16)
        o_ref[...] = x

    h = jnp.maximum(_dot(hn_ref[...], wu_ref[...]), 0.0)
    o_ref[...] += _dot((h * h).astype(BF16), wd_ref[...])


def _mlp_first_kernel(x_ref, g_ref, wu32_ref, wd32_ref, o_ref, wu_ref, wd_ref, hn_ref):
    k = pl.program_id(0)

    @pl.when(k == 0)
    def _():
        x = x_ref[...]
        hn_ref[...] = _rms(x, g_ref[...]).astype(BF16)
        o_ref[...] = x

    wu = wu32_ref[...].astype(BF16)
    wd = wd32_ref[...].astype(BF16)
    wu_ref[...] = wu
    wd_ref[...] = wd
    h = jnp.maximum(_dot(hn_ref[...], wu), 0.0)
    o_ref[...] += _dot((h * h).astype(BF16), wd)


def _mlp_rest_kernel(x_ref, g_ref, wu_ref, wd_ref, first_ref, o_ref, hn_ref):
    del first_ref
    _mlp_kernel(x_ref, g_ref, wu_ref, wd_ref, o_ref, hn_ref)


def _mlp(x, gain, w_up_f32, w_down_f32, layer):
    t, d = x.shape
    dff = w_up_f32.shape[2]
    tm = _tile(t, 768)
    tf1 = _tile(dff, 512, LANES)
    o_first, w_up, w_down = pl.pallas_call(
        _mlp_first_kernel,
        grid=(dff // tf1,),
        in_specs=[pl.BlockSpec((tm, d), lambda k: (0, 0)),
                  pl.BlockSpec((1, d), lambda k: (0, 0)),
                  pl.BlockSpec((None, d, tf1), lambda k: (layer, 0, k)),
                  pl.BlockSpec((None, tf1, d), lambda k: (layer, k, 0))],
        out_specs=[pl.BlockSpec((tm, d), lambda k: (0, 0)),
                   pl.BlockSpec((d, tf1), lambda k: (0, k)),
                   pl.BlockSpec((tf1, d), lambda k: (k, 0))],
        out_shape=[jax.ShapeDtypeStruct((t, d), F32),
                   jax.ShapeDtypeStruct((d, dff), BF16),
                   jax.ShapeDtypeStruct((dff, d), BF16)],
        scratch_shapes=[pltpu.VMEM((tm, d), BF16)],
        compiler_params=_params("arbitrary"),
        name="mlp_first",
    )(x, gain, w_up_f32, w_down_f32)
    if t == tm:
        return o_first
    tf = _tile(dff, 1024, LANES)
    return pl.pallas_call(
        _mlp_rest_kernel,
        grid=(t // tm - 1, dff // tf),
        in_specs=[pl.BlockSpec((tm, d), lambda i, k: (i + 1, 0)),
                  pl.BlockSpec((1, d), lambda i, k: (0, 0)),
                  pl.BlockSpec((d, tf), lambda i, k: (0, k)),
                  pl.BlockSpec((tf, d), lambda i, k: (k, 0)),
                  pl.BlockSpec(memory_space=pl.ANY)],
        out_specs=pl.BlockSpec((tm, d), lambda i, k: (i + 1, 0)),
        out_shape=jax.ShapeDtypeStruct((t, d), F32),
        input_output_aliases={4: 0},
        scratch_shapes=[pltpu.VMEM((tm, d), BF16)],
        compiler_params=_params("arbitrary", "arbitrary"),
        name="mlp",
    )(x, gain, w_up, w_down, o_first)


def _final_norm_kernel(x_ref, g_ref, yp_ref, ys_ref, *, n_prompt_tiles):
    i = pl.program_id(0)
    y = _rms(x_ref[...], g_ref[...])

    @pl.when(i < n_prompt_tiles)
    def _():
        yp_ref[...] = y

    @pl.when(i >= n_prompt_tiles)
    def _():
        ys_ref[...] = y


def _final_norm(x, gain, n_prompt_rows):
    t, d = x.shape
    n_sample_rows = t - n_prompt_rows
    tm = _tile(math.gcd(n_prompt_rows, n_sample_rows), 512)
    npt = n_prompt_rows // tm
    return pl.pallas_call(
        functools.partial(_final_norm_kernel, n_prompt_tiles=npt),
        grid=(t // tm,),
        in_specs=[pl.BlockSpec((tm, d), lambda i: (i, 0)),
                  pl.BlockSpec((1, d), lambda i: (0, 0))],
        out_specs=[pl.BlockSpec((tm, d), lambda i: (jnp.minimum(i, npt - 1), 0)),
                   pl.BlockSpec((tm, d), lambda i: (jnp.maximum(i - npt, 0), 0))],
        out_shape=[jax.ShapeDtypeStruct((n_prompt_rows, d), F32),
                   jax.ShapeDtypeStruct((n_sample_rows, d), F32)],
        compiler_params=_params("arbitrary"),
        name="final_norm",
    )(x, gain)


def _bias_kernel(t_ref, o_ref):
    n_heads = o_ref.shape[0]
    width = t_ref.shape[1]
    r = lax.broadcasted_iota(jnp.int32, (UNIT, WIN), 0)
    c = lax.broadcasted_iota(jnp.int32, (UNIT, WIN), 1)
    dist = r // PAIR + KV_TAIL // PAIR - c // PAIR
    q_hi = (r % PAIR) >= CHUNK
    k_hi = (c % PAIR) >= CHUNK
    masked = ((dist < 0) | (dist > KV_TAIL // PAIR)
              | ((dist == 0) & jnp.logical_not(q_hi) & k_hi)
              | ((dist == KV_TAIL // PAIR) & q_hi & jnp.logical_not(k_hi)))
    for h in range(n_heads):
        rows = jnp.broadcast_to(t_ref[h:h + 1, :], (UNIT, width))
        rolled = pltpu.roll(rows, width - (UNIT - 1), 1, stride=1, stride_axis=0)
        o_ref[h] = jnp.where(masked, NEG_INF, rolled[:, :WIN] * LOG2_E)


def _bias_table(rel_bias):
    n_heads, n_rel = rel_bias.shape
    max_rel = (n_rel - 1) // 2
    assert max_rel == PAIR
    width = WIN + UNIT
    n_far = WIN - 1 - max_rel + 1
    tvec = jnp.concatenate(
        [jnp.broadcast_to(rel_bias[:, 2 * max_rel:], (n_heads, n_far)),
         jnp.flip(rel_bias[:, :2 * max_rel], axis=1),
         jnp.broadcast_to(rel_bias[:, :1], (n_heads, width - n_far - 2 * max_rel))], axis=1)
    return pl.pallas_call(
        _bias_kernel,
        out_shape=jax.ShapeDtypeStruct((n_heads, UNIT, WIN), F32),
        compiler_params=pltpu.CompilerParams(vmem_limit_bytes=VMEM_LIMIT_BYTES),
        name="attn_bias",
    )(tvec)


def _softmax_pv(scores, values):
    dh = values[0].shape[1]
    m = scores[0].max(axis=1, keepdims=True)
    for s in scores[1:]:
        m = jnp.maximum(m, s.max(axis=1, keepdims=True))
    acc = None
    for s, v in zip(scores, values):
        p = jnp.exp2(s - m).astype(BF16)
        r = _dot(p, jnp.concatenate([v, jnp.ones_like(v)], axis=1))
        acc = r if acc is None else acc + r
    return acc[:, :dh] / acc[:, dh:]


def _attn_prompt_kernel(q_ref, k_ref, v_ref, b_ref, o_ref, *, n_heads, dh, n_units):
    qi = pl.program_id(2)

    def run(items):
        def raw_scores(item):
            q_rows, h, key_rows, _, _ = item
            hs = slice(h * dh, (h + 1) * dh)
            return _dot_nt(q_ref[q_rows, hs], k_ref[key_rows, hs])

        s_next = raw_scores(items[0])
        for n, item in enumerate(items):
            s_raw = s_next
            if n + 1 < len(items):
                s_next = raw_scores(items[n + 1])
            q_rows, h, key_rows, _, b_cols = item
            hs = slice(h * dh, (h + 1) * dh)
            o = _softmax_pv([s_raw + b_ref[h, :, b_cols]], [v_ref[key_rows, hs]])
            o_ref[q_rows, hs] = o.astype(BF16)

    @pl.when(qi == 0)
    def _():
        items = []
        for u in range(n_units):
            nk = (u + 1) * UNIT
            for h in range(n_heads):
                items.append((slice(u * UNIT, (u + 1) * UNIT), h, slice(0, nk), nk, slice(WIN - nk, WIN)))
        run(items)

    @pl.when(qi > 0)
    def _():
        items = []
        for u in range(n_units):
            k0 = pl.multiple_of((qi * n_units + u) * UNIT - KV_TAIL, PAIR)
            for h in range(n_heads):
                items.append((slice(u * UNIT, (u + 1) * UNIT), h, pl.ds(k0, WIN), WIN, slice(0, WIN)))
        run(items)


def _attn_prompt(qkv, bias, n_batches, seq, d, dh):
    n_units = KV_TAIL // UNIT
    tq = n_units * UNIT
    assert seq % tq == 0 and seq >= WIN
    hw = min(d, 4 * dh)
    n_hg = d // hw
    n_heads = hw // dh
    return pl.pallas_call(
        functools.partial(_attn_prompt_kernel, n_heads=n_heads, dh=dh, n_units=n_units),
        grid=(n_batches, n_hg, seq // tq),
        in_specs=[pl.BlockSpec((tq, hw), lambda b, g, i: (b * (seq // tq) + i, g)),
                  pl.BlockSpec((seq, hw), lambda b, g, i: (b, n_hg + g)),
                  pl.BlockSpec((seq, hw), lambda b, g, i: (b, 2 * n_hg + g)),
                  pl.BlockSpec((n_heads, UNIT, WIN), lambda b, g, i: (g, 0, 0))],
        out_specs=pl.BlockSpec((tq, hw), lambda b, g, i: (b * (seq // tq) + i, g)),
        out_shape=jax.ShapeDtypeStruct((n_batches * seq, d), BF16),
        compiler_params=_params("arbitrary", "arbitrary", "arbitrary"),
        name="attn_prompt",
    )(qkv, qkv, qkv, bias)


def _attn_sample_kernel(q_ref, kn_ref, vn_ref, kc_ref, vc_ref, b_ref, o_ref, *, n_heads, dh):
    ts = q_ref.shape[0]
    lc = kc_ref.shape[0] // n_heads

    def cached(ref, h):
        return ref[pl.ds(h, lc, stride=n_heads), :].astype(BF16)

    def raw_scores(h):
        hs = slice(h * dh, (h + 1) * dh)
        q = q_ref[:, hs]
        return _dot_nt(q, cached(kc_ref, h)), _dot_nt(q, kn_ref[:, hs])

    s_next = raw_scores(0)
    for h in range(n_heads):
        s_c, s_n = s_next
        if h + 1 < n_heads:
            s_next = raw_scores(h + 1)
        hs = slice(h * dh, (h + 1) * dh)
        o = _softmax_pv([s_c + b_ref[h, :ts, :lc], s_n + b_ref[h, :ts, lc:lc + ts]],
                        [cached(vc_ref, h), vn_ref[:, hs]])
        o_ref[:, hs] = o.astype(BF16)


def _attn_sample(qkv, k_cache, v_cache, layer, bias, row0, d, dh):
    _, n_b, lc, n_heads, _ = k_cache.shape
    ts = CHUNK
    assert lc == KV_TAIL and row0 % ts == 0 and n_heads * dh == d
    r0 = row0 // ts
    cache_shape = (k_cache.shape[0], n_b, lc * n_heads, dh)
    cache_spec = pl.BlockSpec((None, None, lc * n_heads, dh), lambda b: (layer, b, 0, 0))
    return pl.pallas_call(
        functools.partial(_attn_sample_kernel, n_heads=n_heads, dh=dh),
        grid=(n_b,),
        in_specs=[pl.BlockSpec((ts, d), lambda b: (r0 + b, 0)),
                  pl.BlockSpec((ts, d), lambda b: (r0 + b, 1)),
                  pl.BlockSpec((ts, d), lambda b: (r0 + b, 2)),
                  cache_spec, cache_spec,
                  pl.BlockSpec((n_heads, UNIT, WIN), lambda b: (0, 0, 0))],
        out_specs=pl.BlockSpec((ts, d), lambda b: (b, 0)),
        out_shape=jax.ShapeDtypeStruct((n_b * ts, d), BF16),
        compiler_params=_params("arbitrary"),
        name="attn_sample",
    )(qkv, qkv, qkv, k_cache.reshape(cache_shape), v_cache.reshape(cache_shape), bias)


def _norm_blocked_kernel(x_ref, g_ref, o_ref, slab_scr):
    n_gb = o_ref.shape[0]
    tm = x_ref.shape[0]
    hn = _rms(x_ref[...], g_ref[...])
    for gb in range(n_gb):
        slab_scr[gb * tm:(gb + 1) * tm, :] = hn[:, gb * LANES:(gb + 1) * LANES]
    for gb in range(n_gb):
        for s in range(SSM_L):
            rows = pl.ds(gb * tm + s, tm // SSM_L, stride=SSM_L)
            o_ref[gb, :, s * LANES:(s + 1) * LANES] = slab_scr[rows, :].astype(BF16)


def _norm_blocked(x, gain):
    t, d = x.shape
    nc = t // SSM_L
    tm = _tile(t, 768, SSM_L * SUBLANES_BF16)
    n_gb = d // LANES
    return pl.pallas_call(
        _norm_blocked_kernel,
        grid=(t // tm,),
        in_specs=[pl.BlockSpec((tm, d), lambda i: (i, 0)),
                  pl.BlockSpec((1, d), lambda i: (0, 0))],
        out_specs=pl.BlockSpec((n_gb, tm // SSM_L, SSM_L * LANES), lambda i: (0, i, 0)),
        out_shape=jax.ShapeDtypeStruct((n_gb, nc, SSM_L * LANES), BF16),
        scratch_shapes=[pltpu.VMEM((n_gb * tm, LANES), F32)],
        compiler_params=_params("arbitrary"),
        name="ssm_norm",
    )(x, gain)


def _gelu_tanh(x):
    cdf = 0.5 * (1.0 + jnp.tanh(math.sqrt(2.0 / math.pi) * (x + 0.044715 * (x ** 3))))
    return x * cdf


def _ssm_core_kernel(u_ref, x0_ref, prow_ref, bxr_ref, bxi_ref, cxr_ref, cxi_ref, d_ref,
                     g_ref, xfp_ref, xfs_ref,
                     w_scr, cmt_scr, toep_scr, s_scr, xst_scr,
                     *, n_prompt, blocks_per_prompt, n_sample, blocks_per_sample, row_tile,
                     groups_per_tile):
    n_state = bxr_ref.shape[1]
    ch = bxr_ref.shape[0]

    def discretise(a_re, a_im, log_dt):
        dt = jnp.exp(log_dt)
        mag = jnp.exp(dt * a_re)
        l_re = mag * jnp.cos(dt * a_im)
        l_im = mag * jnp.sin(dt * a_im)
        den = a_re * a_re + a_im * a_im
        n_re = l_re - 1.0
        n_im = l_im
        z_re = (n_re * a_re + n_im * a_im) / den
        z_im = (n_im * a_re - n_re * a_im) / den
        return l_re, l_im, z_re, z_im

    def powers(l_re, l_im, n):
        out = [(jnp.ones_like(l_re), jnp.zeros_like(l_im))]
        for _ in range(n):
            pr, pi = out[-1]
            out.append((pr * l_re - pi * l_im, pr * l_im + pi * l_re))
        return out

    lr, li, zr, zi = discretise(prow_ref[0:1, :], prow_ref[1:2, :], prow_ref[2:3, :])
    prow = powers(lr, li, SSM_L)

    p = n_state // groups_per_tile
    sg = ch // groups_per_tile
    mb = (lax.broadcasted_iota(jnp.int32, (ch, n_state), 0) // sg ==
          lax.broadcasted_iota(jnp.int32, (ch, n_state), 1) // p)

    bxr = bxr_ref[...]
    bxi = bxi_ref[...]
    bb_r = jnp.where(mb, zr * bxr - zi * bxi, 0.0)
    bb_i = jnp.where(mb, zr * bxi + zi * bxr, 0.0)

    for s in range(SSM_L):
        pr, pi = prow[SSM_L - 1 - s]
        rows = slice(s * ch, (s + 1) * ch)
        w_scr[rows, 0:n_state] = (pr * bb_r - pi * bb_i).astype(BF16)
        w_scr[rows, n_state:2 * n_state] = (pr * bb_i + pi * bb_r).astype(BF16)

    cxr = jnp.where(mb, cxr_ref[...], 0.0)
    cxi = jnp.where(mb, cxi_ref[...], 0.0)
    for tau in range(SSM_L + 1):
        pr, pi = prow[tau]
        rows = slice(tau * ch, (tau + 1) * ch)
        cmt_scr[rows, 0:n_state] = (cxr * pr - cxi * pi).astype(BF16)
        cmt_scr[rows, n_state:2 * n_state] = (-(cxr * pi + cxi * pr)).astype(BF16)

    width = SSM_L * ch
    bbar = jnp.concatenate([bb_r.astype(BF16), bb_i.astype(BF16)], axis=1)
    k_all = _dot_nt(bbar, cmt_scr[0:width, :]).astype(BF16)
    for s in range(SSM_L):
        rows = slice(s * ch, (s + 1) * ch)
        if s % 2 == 1:
            toep_scr[rows, (s - 1) * ch:s * ch] = jnp.zeros((ch, ch), BF16)
        toep_scr[rows, s * ch:width] = k_all[:, 0:width - s * ch]

    n_rows = u_ref.shape[0]
    for r in range(n_rows // row_tile):
        rows = slice(r * row_tile, (r + 1) * row_tile)
        s_scr[rows, :] = _dot(u_ref[rows, :], w_scr[...])

    l16r, l16i = prow[SSM_L]

    def step(xr, xi, sr, si):
        return l16r * xr - l16i * xi + sr, l16r * xi + l16i * xr + si

    def prompt_body(c, carry):
        new = []
        for b in range(n_prompt):
            xr, xi = carry[b]
            row = b * blocks_per_prompt + c
            xst_scr[pl.ds(row, 1), 0:n_state] = xr
            xst_scr[pl.ds(row, 1), n_state:2 * n_state] = xi
            new.append(step(xr, xi, s_scr[pl.ds(row, 1), 0:n_state],
                            s_scr[pl.ds(row, 1), n_state:2 * n_state]))
        return tuple(new)

    zero = jnp.zeros((1, n_state), F32)
    fin = lax.fori_loop(0, blocks_per_prompt, prompt_body, tuple((zero, zero) for _ in range(n_prompt)))
    for b in range(n_prompt):
        xfp_ref[b:b + 1, 0:n_state] = fin[b][0]
        xfp_ref[b:b + 1, n_state:2 * n_state] = fin[b][1]

    base = n_prompt * blocks_per_prompt
    for b in range(n_sample):
        xr = x0_ref[b:b + 1, 0:n_state]
        xi = x0_ref[b:b + 1, n_state:2 * n_state]
        for j in range(blocks_per_sample):
            row = base + b * blocks_per_sample + j
            xst_scr[row:row + 1, 0:n_state] = xr
            xst_scr[row:row + 1, n_state:2 * n_state] = xi
            xr, xi = step(xr, xi, s_scr[row:row + 1, 0:n_state], s_scr[row:row + 1, n_state:2 * n_state])
        xfs_ref[b:b + 1, 0:n_state] = xr
        xfs_ref[b:b + 1, n_state:2 * n_state] = xi

    dvec = d_ref[...]
    for r in range(n_rows // row_tile):
        rows = slice(r * row_tile, (r + 1) * row_tile)
        xst = xst_scr[rows, :].astype(BF16)
        for t in range(0, SSM_L, 2):
            k_ext = (t + 2) * ch
            y = (_dot(u_ref[rows, 0:k_ext], toep_scr[0:k_ext, t * ch:(t + 2) * ch])
                 + _dot_nt(xst, cmt_scr[(t + 1) * ch:(t + 3) * ch, :]))
            for tt in (t, t + 1):
                cols = slice(tt * ch, (tt + 1) * ch)
                yt = y[:, (tt - t) * ch:(tt - t + 1) * ch] + dvec * u_ref[rows, cols].astype(F32)
                g_ref[rows, cols] = _gelu_tanh(yt).astype(BF16)


def _ssm_core(u_blk, x0, prow, bxr, bxi, cxr, cxi, dvec, n_prompt, blocks_per_prompt, n_sample,
              blocks_per_sample, groups_per_tile):
    n_gb, nc, width = u_blk.shape
    ch = LANES
    n_state = bxr.shape[2]
    assert width == SSM_L * ch and nc == n_prompt * blocks_per_prompt + n_sample * blocks_per_sample
    row_tile = _tile(nc, 384)
    kern = functools.partial(_ssm_core_kernel, n_prompt=n_prompt, blocks_per_prompt=blocks_per_prompt,
                             n_sample=n_sample, blocks_per_sample=blocks_per_sample, row_tile=row_tile,
                             groups_per_tile=groups_per_tile)
    return pl.pallas_call(
        kern,
        grid=(n_gb,),
        in_specs=[pl.BlockSpec((None, nc, width), lambda g: (g, 0, 0)),
                  pl.BlockSpec((n_sample, 2 * n_state), lambda g: (0, g)),
                  pl.BlockSpec((None, 3, n_state), lambda g: (g, 0, 0)),
                  pl.BlockSpec((None, ch, n_state), lambda g: (g, 0, 0)),
                  pl.BlockSpec((None, ch, n_state), lambda g: (g, 0, 0)),
                  pl.BlockSpec((None, ch, n_state), lambda g: (g, 0, 0)),
                  pl.BlockSpec((None, ch, n_state), lambda g: (g, 0, 0)),
                  pl.BlockSpec((1, ch), lambda g: (0, g))],
        out_specs=[pl.BlockSpec((None, nc, width), lambda g: (g, 0, 0)),
                   pl.BlockSpec((n_prompt, 2 * n_state), lambda g: (0, g)),
                   pl.BlockSpec((n_sample, 2 * n_state), lambda g: (0, g))],
        out_shape=[jax.ShapeDtypeStruct((n_gb, nc, width), BF16),
                   jax.ShapeDtypeStruct((n_prompt, n_gb * 2 * n_state), F32),
                   jax.ShapeDtypeStruct((n_sample, n_gb * 2 * n_state), F32)],
        scratch_shapes=[pltpu.VMEM((width, 2 * n_state), BF16),
                        pltpu.VMEM((width + ch, 2 * n_state), BF16),
                        pltpu.VMEM((width, width), BF16),
                        pltpu.VMEM((nc, 2 * n_state), F32),
                        pltpu.VMEM((nc, 2 * n_state), F32)],
        compiler_params=_params("arbitrary"),
        name="ssm_core",
    )(u_blk, x0, prow, bxr, bxi, cxr, cxi, dvec)


def _glu_res_kernel(g_ref, x_ref, wa_ref, wb_ref, o_ref, slab_scr, lhs_scr):
    n_gb = g_ref.shape[0]
    tm = x_ref.shape[0]

    for gb in range(n_gb):
        for s in range(SSM_L):
            rows = pl.ds(gb * tm + s, tm // SSM_L, stride=SSM_L)
            slab_scr[rows, :] = g_ref[gb, :, s * LANES:(s + 1) * LANES].astype(F32)
    for gb in range(n_gb):
        lhs_scr[:, gb * LANES:(gb + 1) * LANES] = slab_scr[gb * tm:(gb + 1) * tm, :].astype(BF16)

    g = lhs_scr[...]
    a = _dot(g, wa_ref[...])
    b = _dot(g, wb_ref[...])
    o_ref[...] = x_ref[...] + a * jax.nn.sigmoid(b)


def _glu_res(x, g_blk, w_a_all, w_b_all, layer):
    t, d = x.shape
    n_gb, nc, _ = g_blk.shape
    tm = _tile(t, 512, SSM_L * SUBLANES_BF16)
    w_spec = pl.BlockSpec((None, d, d), lambda i: (layer, 0, 0), pipeline_mode=pl.Buffered(1))
    return pl.pallas_call(
        _glu_res_kernel,
        grid=(t // tm,),
        in_specs=[pl.BlockSpec((n_gb, tm // SSM_L, SSM_L * LANES), lambda i: (0, i, 0)),
                  pl.BlockSpec((tm, d), lambda i: (i, 0)),
                  w_spec, w_spec],
        out_specs=pl.BlockSpec((tm, d), lambda i: (i, 0)),
        out_shape=jax.ShapeDtypeStruct((t, d), F32),
        scratch_shapes=[pltpu.VMEM((n_gb * tm, LANES), F32),
                        pltpu.VMEM((tm, d), BF16)],
        compiler_params=_params("arbitrary"),
        name="glu_res",
    )(g_blk, x, w_a_all, w_b_all)


def _ssm_layouts(a_re, a_im, log_dt, b_re, b_im, c_re, c_im):
    g, p = a_re.shape
    sg = b_re.shape[2]
    gl = LANES // sg
    n_gb = g // gl
    dt = jnp.broadcast_to(log_dt[:, None], (g, p))
    prow = jnp.stack([a_re, a_im, dt], axis=0).reshape(3, n_gb, gl * p).transpose(1, 0, 2)

    def expand_b(b):
        bt = b.reshape(n_gb, gl, p, sg).transpose(0, 3, 1, 2)
        bt = jnp.broadcast_to(bt[:, None], (n_gb, gl, sg, gl, p))
        return bt.reshape(n_gb, gl * sg, gl * p)

    def expand_c(c):
        ct = c.reshape(n_gb, gl, sg, 1, p)
        ct = jnp.broadcast_to(ct, (n_gb, gl, sg, gl, p))
        return ct.reshape(n_gb, gl * sg, gl * p)

    return prow, expand_b(b_re), expand_b(b_im), expand_c(c_re), expand_c(c_im)


def _state_to_tiles(st_re, st_im, gl):
    b, g, p = st_re.shape
    n_gb = g // gl
    st = jnp.stack([st_re.reshape(b, n_gb, gl * p), st_im.reshape(b, n_gb, gl * p)], axis=2)
    return st.reshape(b, n_gb * 2 * gl * p)


def _tiles_to_state(x, g, p, gl):
    b = x.shape[0]
    n_gb = g // gl
    x = x.reshape(b, n_gb, 2, gl, p)
    return x[:, :, 0].reshape(b, g, p), x[:, :, 1].reshape(b, g, p)


def kernel(x_prompt, x_sample, cache_attn_k, cache_attn_v, state_ssm_re, state_ssm_im, norm_mix, norm_mlp, norm_final, attn_w_qkv, attn_w_o, attn_rel_bias, ssm_a_re, ssm_a_im, ssm_log_dt, ssm_b_re, ssm_b_im, ssm_c_re, ssm_c_im, ssm_d, ssm_w_glu_a, ssm_w_glu_b, mlp_w_up, mlp_w_down):
    bp, seq, d = x_prompt.shape
    bs, ts, _ = x_sample.shape
    depth = norm_mix.shape[0]
    n_heads, dh = cache_attn_k.shape[3], cache_attn_k.shape[4]
    n_groups, p_state = ssm_a_re.shape[1], ssm_a_re.shape[2]
    sg = d // n_groups
    gl = LANES // sg
    assert ts == CHUNK and seq % KV_TAIL == 0 and (bs * ts) % KV_TAIL == 0
    assert seq % SSM_L == 0 and ts % SSM_L == 0 and d % LANES == 0

    n_prompt_rows = bp * seq
    xs = (x_prompt.reshape(n_prompt_rows, d), x_sample.reshape(bs * ts, d))

    w_qkv = attn_w_qkv.astype(BF16)
    w_o = attn_w_o.astype(BF16)
    w_glu_a = ssm_w_glu_a.astype(BF16)
    w_glu_b = ssm_w_glu_b.astype(BF16)

    k_p, v_p, k_s, v_s = [], [], [], []
    sr_p, si_p, sr_s, si_s = [], [], [], []
    for i in range(depth):
        j = i // 2
        gain = norm_mix[i][None, :]
        if i % 2 == 0:
            qkv, tail = _qkv(xs, gain, w_qkv, j, seq // KV_TAIL, bp, dh ** -0.5 * LOG2_E)
            bias = _bias_table(attn_rel_bias[j])
            o_p = _attn_prompt(qkv, bias, bp, seq, d, dh)
            o_s = _attn_sample(qkv, cache_attn_k, cache_attn_v, j, bias, n_prompt_rows, d, dh)
            x = _proj_res(xs, o_p, o_s, w_o, j)
            tail_p = tail[:bp * KV_TAIL].reshape(bp, KV_TAIL, 3, n_heads, dh)
            tail_s = tail[bp * KV_TAIL:].reshape(bs, ts, 3, n_heads, dh)
            k_p.append(tail_p[:, :, 1]); v_p.append(tail_p[:, :, 2])
            k_s.append(tail_s[:, :, 1]); v_s.append(tail_s[:, :, 2])
        else:
            x = xs[0]
            layouts = _ssm_layouts(ssm_a_re[j], ssm_a_im[j], ssm_log_dt[j], ssm_b_re[j], ssm_b_im[j],
                                   ssm_c_re[j], ssm_c_im[j])
            u_blk = _norm_blocked(x, gain)
            x0 = _state_to_tiles(state_ssm_re[j], state_ssm_im[j], gl)
            g_blk, xf_p, xf_s = _ssm_core(u_blk, x0, *layouts, ssm_d[j][None, :], bp, seq // SSM_L, bs,
                                          ts // SSM_L, gl)
            x = _glu_res(x, g_blk, w_glu_a, w_glu_b, j)
            rp, ip = _tiles_to_state(xf_p, n_groups, p_state, gl)
            rs, is_ = _tiles_to_state(xf_s, n_groups, p_state, gl)
            sr_p.append(rp); si_p.append(ip); sr_s.append(rs); si_s.append(is_)
        x = _mlp(x, norm_mlp[i][None, :], mlp_w_up, mlp_w_down, i)
        xs = (x,)

    y_p, y_s = _final_norm(xs[0], norm_final[None, :], n_prompt_rows)
    return (y_p.reshape(bp, seq, d), y_s.reshape(bs, ts, d),
            jnp.stack(k_p), jnp.stack(v_p), jnp.stack(sr_p), jnp.stack(si_p),
            jnp.stack(k_s), jnp.stack(v_s), jnp.stack(sr_s), jnp.stack(si_s))
```

```python
import functools
import math

import jax
import jax.numpy as jnp
from jax import lax
from jax.experimental import pallas as pl
from jax.experimental.pallas import tpu as pltpu

CHUNK = 64
N_PAST_CHUNKS = 8
RMS_EPS = 1e-5
NEG_INF = -1e30
LOG2_E = math.log2(math.e)

LANES = 128
SUBLANES_BF16 = 16
VMEM_LIMIT_BYTES = 58 * 1024 * 1024

PAIR = 2 * CHUNK
UNIT = 2 * PAIR
KV_TAIL = CHUNK * N_PAST_CHUNKS
WIN = KV_TAIL + UNIT
SSM_L = 16

F32 = jnp.float32
BF16 = jnp.bfloat16


def _params(*sem):
    return pltpu.CompilerParams(dimension_semantics=sem, vmem_limit_bytes=VMEM_LIMIT_BYTES)


def _tile(n, target, mult=SUBLANES_BF16):
    best = None
    for t in range(mult, min(n, target) + 1, mult):
        if n % t == 0:
            best = t
    assert best is not None, (n, target, mult)
    return best


def _rms(x, g):
    y = x * lax.rsqrt(jnp.mean(x * x, axis=-1, keepdims=True) + RMS_EPS)
    return y * g


def _dot(a, b):
    return jnp.dot(a, b, preferred_element_type=F32)


def _dot_nt(a, b):
    return lax.dot_general(a, b, (((1,), (1,)), ((), ())), preferred_element_type=F32)


def _x_specs(xs, tm, d, n_prompt_tiles, grid_rank, row_axis):
    def at(f):
        return lambda *idx: (f(idx[row_axis]), 0)
    if len(xs) == 1:
        return [pl.BlockSpec((tm, d), at(lambda i: i))]
    return [pl.BlockSpec((tm, d), at(lambda i: jnp.minimum(i, n_prompt_tiles - 1))),
            pl.BlockSpec((tm, d), at(lambda i: jnp.maximum(i - n_prompt_tiles, 0)))]


def _read_x(x_refs, i, n_prompt_tiles):
    if len(x_refs) == 1:
        return x_refs[0][...]
    return jnp.where(i < n_prompt_tiles, x_refs[0][...], x_refs[1][...])


def _qkv_kernel(*refs, n_src, n_prompt_tiles, q_scale):
    x_refs, (g_ref, w_ref, o_ref, ktail_ref, vtail_ref) = refs[:n_src], refs[n_src:]
    x = _read_x(x_refs, pl.program_id(1), n_prompt_tiles)
    hn = _rms(x, g_ref[...]).astype(BF16)
    acc = _dot(hn, w_ref[...])
    scale = jnp.where(pl.program_id(0) == 0, q_scale, 1.0).astype(F32)
    o_ref[...] = (acc * scale).astype(BF16)
    ktail_ref[...] = acc
    vtail_ref[...] = acc


def _qkv(xs, gain, w_all, layer, n_prompt_tiles_per_batch, n_prompt_batches, q_scale):
    d = xs[0].shape[1]
    t = sum(x.shape[0] for x in xs)
    n = w_all.shape[2]
    tm = KV_TAIL
    tn = d
    assert n == 3 * tn
    nt = t // tm
    n_prompt_tiles = n_prompt_tiles_per_batch * n_prompt_batches
    n_slots = n_prompt_batches + (nt - n_prompt_tiles)

    def slot(i):
        return jnp.where(i < n_prompt_tiles, i // n_prompt_tiles_per_batch,
                         i - n_prompt_tiles + n_prompt_batches)

    def tail_block(col):
        return lambda j, i: (jnp.where(j == col, slot(i), jnp.where(j < col, n_slots, n_slots + 1)), 0)

    return pl.pallas_call(
        functools.partial(_qkv_kernel, n_src=len(xs), n_prompt_tiles=n_prompt_tiles, q_scale=q_scale),
        grid=(n // tn, nt),
        in_specs=_x_specs(xs, tm, d, n_prompt_tiles, 2, 1) + [
            pl.BlockSpec((1, d), lambda j, i: (0, 0)),
            pl.BlockSpec((None, d, tn), lambda j, i: (layer, 0, j))],
        out_specs=[pl.BlockSpec((tm, tn), lambda j, i: (i, j)),
                   pl.BlockSpec((tm, tn), tail_block(1)),
                   pl.BlockSpec((tm, tn), tail_block(2))],
        out_shape=[jax.ShapeDtypeStruct((t, n), BF16),
                   jax.ShapeDtypeStruct(((n_slots + 2) * tm, tn), F32),
                   jax.ShapeDtypeStruct(((n_slots + 2) * tm, tn), F32)],
        compiler_params=_params("arbitrary", "arbitrary"),
        name="qkv",
    )(*xs, gain, w_all)


def _proj_res_kernel(*refs, n_src, n_prompt_tiles):
    x_refs, a_refs, (w_ref, o_ref) = refs[:n_src], refs[n_src:n_src + 2], refs[n_src + 2:]
    i = pl.program_id(0)
    o_ref[...] = _read_x(x_refs, i, n_prompt_tiles) + _dot(_read_x(a_refs, i, n_prompt_tiles), w_ref[...])


def _proj_res(xs, a_prompt, a_sample, w_all, layer):
    d = xs[0].shape[1]
    n_prompt_rows, k = a_prompt.shape
    t = n_prompt_rows + a_sample.shape[0]
    tm = _tile(math.gcd(n_prompt_rows, t - n_prompt_rows), 512)
    npt = n_prompt_rows // tm
    return pl.pallas_call(
        functools.partial(_proj_res_kernel, n_src=len(xs), n_prompt_tiles=npt),
        grid=(t // tm,),
        in_specs=_x_specs(xs, tm, d, npt, 1, 0) + _x_specs((a_prompt, a_sample), tm, k, npt, 1, 0) + [
            pl.BlockSpec((None, k, d), lambda i: (layer, 0, 0), pipeline_mode=pl.Buffered(1))],
        out_specs=pl.BlockSpec((tm, d), lambda i: (i, 0)),
        out_shape=jax.ShapeDtypeStruct((t, d), F32),
        compiler_params=_params("arbitrary"),
        name="proj_res",
    )(*xs, a_prompt, a_sample, w_all)


def _mlp_kernel(x_ref, g_ref, wu_ref, wd_ref, o_ref, hn_ref):
    k = pl.program_id(1)

    @pl.when(k == 0)
    def _():
        x = x_ref[...]
        hn_ref[...] = _rms(x, g_ref[...]).astype(BF16)
        o_ref[...] = x

    h = jnp.maximum(_dot(hn_ref[...], wu_ref[...]), 0.0)
    o_ref[...] += _dot((h * h).astype(BF16), wd_ref[...])


def _mlp_first_kernel(x_ref, g_ref, wu32_ref, wd32_ref, o_ref, wu_ref, wd_ref, hn_ref):
    k = pl.program_id(0)

    @pl.when(k == 0)
    def _():
        x = x_ref[...]
        hn_ref[...] = _rms(x, g_ref[...]).astype(BF16)
        o_ref[...] = x

    wu = wu32_ref[...].astype(BF16)
    wd = wd32_ref[...].astype(BF16)
    wu_ref[...] = wu
    wd_ref[...] = wd
    h = jnp.maximum(_dot(hn_ref[...], wu), 0.0)
    o_ref[...] += _dot((h * h).astype(BF16), wd)


def _mlp_rest_kernel(x_ref, g_ref, wu_ref, wd_ref, first_ref, o_ref, hn_ref):
    del first_ref
    _mlp_kernel(x_ref, g_ref, wu_ref, wd_ref, o_ref, hn_ref)


def _mlp(x, gain, w_up_f32, w_down_f32, layer):
    t, d = x.shape
    dff = w_up_f32.shape[2]
    tm = _tile(t, 768)
    tf1 = _tile(dff, 512, LANES)
    o_first, w_up, w_down = pl.pallas_call(
        _mlp_first_kernel,
        grid=(dff // tf1,),
        in_specs=[pl.BlockSpec((tm, d), lambda k: (0, 0)),
                  pl.BlockSpec((1, d), lambda k: (0, 0)),
                  pl.BlockSpec((None, d, tf1), lambda k: (layer, 0, k)),
                  pl.BlockSpec((None, tf1, d), lambda k: (layer, k, 0))],
        out_specs=[pl.BlockSpec((tm, d), lambda k: (0, 0)),
                   pl.BlockSpec((d, tf1), lambda k: (0, k)),
                   pl.BlockSpec((tf1, d), lambda k: (k, 0))],
        out_shape=[jax.ShapeDtypeStruct((t, d), F32),
                   jax.ShapeDtypeStruct((d, dff), BF16),
                   jax.ShapeDtypeStruct((dff, d), BF16)],
        scratch_shapes=[pltpu.VMEM((tm, d), BF16)],
        compiler_params=_params("arbitrary"),
        name="mlp_first",
    )(x, gain, w_up_f32, w_down_f32)
    if t == tm:
        return o_first
    tf = _tile(dff, 1024, LANES)
    return pl.pallas_call(
        _mlp_rest_kernel,
        grid=(t // tm - 1, dff // tf),
        in_specs=[pl.BlockSpec((tm, d), lambda i, k: (i + 1, 0)),
                  pl.BlockSpec((1, d), lambda i, k: (0, 0)),
                  pl.BlockSpec((d, tf), lambda i, k: (0, k)),
                  pl.BlockSpec((tf, d), lambda i, k: (k, 0)),
                  pl.BlockSpec(memory_space=pl.ANY)],
        out_specs=pl.BlockSpec((tm, d), lambda i, k: (i + 1, 0)),
        out_shape=jax.ShapeDtypeStruct((t, d), F32),
        input_output_aliases={4: 0},
        scratch_shapes=[pltpu.VMEM((tm, d), BF16)],
        compiler_params=_params("arbitrary", "arbitrary"),
        name="mlp",
    )(x, gain, w_up, w_down, o_first)


def _final_norm_kernel(x_ref, g_ref, yp_ref, ys_ref, *, n_prompt_tiles):
    i = pl.program_id(0)
    y = _rms(x_ref[...], g_ref[...])

    @pl.when(i < n_prompt_tiles)
    def _():
        yp_ref[...] = y

    @pl.when(i >= n_prompt_tiles)
    def _():
        ys_ref[...] = y


def _final_norm(x, gain, n_prompt_rows):
    t, d = x.shape
    n_sample_rows = t - n_prompt_rows
    tm = _tile(math.gcd(n_prompt_rows, n_sample_rows), 512)
    npt = n_prompt_rows // tm
    return pl.pallas_call(
        functools.partial(_final_norm_kernel, n_prompt_tiles=npt),
        grid=(t // tm,),
        in_specs=[pl.BlockSpec((tm, d), lambda i: (i, 0)),
                  pl.BlockSpec((1, d), lambda i: (0, 0))],
        out_specs=[pl.BlockSpec((tm, d), lambda i: (jnp.minimum(i, npt - 1), 0)),
                   pl.BlockSpec((tm, d), lambda i: (jnp.maximum(i - npt, 0), 0))],
        out_shape=[jax.ShapeDtypeStruct((n_prompt_rows, d), F32),
                   jax.ShapeDtypeStruct((n_sample_rows, d), F32)],
        compiler_params=_params("arbitrary"),
        name="final_norm",
    )(x, gain)


def _bias_kernel(t_ref, o_ref):
    n_heads = o_ref.shape[0]
    width = t_ref.shape[1]
    r = lax.broadcasted_iota(jnp.int32, (UNIT, WIN), 0)
    c = lax.broadcasted_iota(jnp.int32, (UNIT, WIN), 1)
    dist = r // PAIR + KV_TAIL // PAIR - c // PAIR
    q_hi = (r % PAIR) >= CHUNK
    k_hi = (c % PAIR) >= CHUNK
    masked = ((dist < 0) | (dist > KV_TAIL // PAIR)
              | ((dist == 0) & jnp.logical_not(q_hi) & k_hi)
              | ((dist == KV_TAIL // PAIR) & q_hi & jnp.logical_not(k_hi)))
    for h in range(n_heads):
        rows = jnp.broadcast_to(t_ref[h:h + 1, :], (UNIT, width))
        rolled = pltpu.roll(rows, width - (UNIT - 1), 1, stride=1, stride_axis=0)
        o_ref[h] = jnp.where(masked, NEG_INF, rolled[:, :WIN] * LOG2_E)


def _bias_table(rel_bias):
    n_heads, n_rel = rel_bias.shape
    max_rel = (n_rel - 1) // 2
    assert max_rel == PAIR
    width = WIN + UNIT
    n_far = WIN - 1 - max_rel + 1
    tvec = jnp.concatenate(
        [jnp.broadcast_to(rel_bias[:, 2 * max_rel:], (n_heads, n_far)),
         jnp.flip(rel_bias[:, :2 * max_rel], axis=1),
         jnp.broadcast_to(rel_bias[:, :1], (n_heads, width - n_far - 2 * max_rel))], axis=1)
    return pl.pallas_call(
        _bias_kernel,
        out_shape=jax.ShapeDtypeStruct((n_heads, UNIT, WIN), F32),
        compiler_params=pltpu.CompilerParams(vmem_limit_bytes=VMEM_LIMIT_BYTES),
        name="attn_bias",
    )(tvec)


def _softmax_pv(scores, values):
    dh = values[0].shape[1]
    m = scores[0].max(axis=1, keepdims=True)
    for s in scores[1:]:
        m = jnp.maximum(m, s.max(axis=1, keepdims=True))
    acc = None
    for s, v in zip(scores, values):
        p = jnp.exp2(s - m).astype(BF16)
        r = _dot(p, jnp.concatenate([v, jnp.ones_like(v)], axis=1))
        acc = r if acc is None else acc + r
    return acc[:, :dh] / acc[:, dh:]


def _attn_prompt_kernel(q_ref, k_ref, v_ref, b_ref, o_ref, *, n_heads, dh, n_units):
    qi = pl.program_id(2)

    def run(items):
        def raw_scores(item):
            q_rows, h, key_rows, _, _ = item
            hs = slice(h * dh, (h + 1) * dh)
            return _dot_nt(q_ref[q_rows, hs], k_ref[key_rows, hs])

        s_next = raw_scores(items[0])
        for n, item in enumerate(items):
            s_raw = s_next
            if n + 1 < len(items):
                s_next = raw_scores(items[n + 1])
            q_rows, h, key_rows, _, b_cols = item
            hs = slice(h * dh, (h + 1) * dh)
            o = _softmax_pv([s_raw + b_ref[h, :, b_cols]], [v_ref[key_rows, hs]])
            o_ref[q_rows, hs] = o.astype(BF16)

    @pl.when(qi == 0)
    def _():
        items = []
        for u in range(n_units):
            nk = (u + 1) * UNIT
            for h in range(n_heads):
                items.append((slice(u * UNIT, (u + 1) * UNIT), h, slice(0, nk), nk, slice(WIN - nk, WIN)))
        run(items)

    @pl.when(qi > 0)
    def _():
        items = []
        for u in range(n_units):
            k0 = pl.multiple_of((qi * n_units + u) * UNIT - KV_TAIL, PAIR)
            for h in range(n_heads):
                items.append((slice(u * UNIT, (u + 1) * UNIT), h, pl.ds(k0, WIN), WIN, slice(0, WIN)))
        run(items)


def _attn_prompt(qkv, bias, n_batches, seq, d, dh):
    n_units = KV_TAIL // UNIT
    tq = n_units * UNIT
    assert seq % tq == 0 and seq >= WIN
    hw = min(d, 4 * dh)
    n_hg = d // hw
    n_heads = hw // dh
    return pl.pallas_call(
        functools.partial(_attn_prompt_kernel, n_heads=n_heads, dh=dh, n_units=n_units),
        grid=(n_batches, n_hg, seq // tq),
        in_specs=[pl.BlockSpec((tq, hw), lambda b, g, i: (b * (seq // tq) + i, g)),
                  pl.BlockSpec((seq, hw), lambda b, g, i: (b, n_hg + g)),
                  pl.BlockSpec((seq, hw), lambda b, g, i: (b, 2 * n_hg + g)),
                  pl.BlockSpec((n_heads, UNIT, WIN), lambda b, g, i: (g, 0, 0))],
        out_specs=pl.BlockSpec((tq, hw), lambda b, g, i: (b * (seq // tq) + i, g)),
        out_shape=jax.ShapeDtypeStruct((n_batches * seq, d), BF16),
        compiler_params=_params("arbitrary", "arbitrary", "arbitrary"),
        name="attn_prompt",
    )(qkv, qkv, qkv, bias)


def _attn_sample_kernel(q_ref, kn_ref, vn_ref, kc_ref, vc_ref, b_ref, o_ref, *, n_heads, dh):
    ts = q_ref.shape[0]
    lc = kc_ref.shape[0] // n_heads

    def cached(ref, h):
        return ref[pl.ds(h, lc, stride=n_heads), :].astype(BF16)

    def raw_scores(h):
        hs = slice(h * dh, (h + 1) * dh)
        q = q_ref[:, hs]
        return _dot_nt(q, cached(kc_ref, h)), _dot_nt(q, kn_ref[:, hs])

    s_next = raw_scores(0)
    for h in range(n_heads):
        s_c, s_n = s_next
        if h + 1 < n_heads:
            s_next = raw_scores(h + 1)
        hs = slice(h * dh, (h + 1) * dh)
        o = _softmax_pv([s_c + b_ref[h, :ts, :lc], s_n + b_ref[h, :ts, lc:lc + ts]],
                        [cached(vc_ref, h), vn_ref[:, hs]])
        o_ref[:, hs] = o.astype(BF16)


def _attn_sample(qkv, k_cache, v_cache, layer, bias, row0, d, dh):
    _, n_b, lc, n_heads, _ = k_cache.shape
    ts = CHUNK
    assert lc == KV_TAIL and row0 % ts == 0 and n_heads * dh == d
    r0 = row0 // ts
    cache_shape = (k_cache.shape[0], n_b, lc * n_heads, dh)
    cache_spec = pl.BlockSpec((None, None, lc * n_heads, dh), lambda b: (layer, b, 0, 0))
    return pl.pallas_call(
        functools.partial(_attn_sample_kernel, n_heads=n_heads, dh=dh),
        grid=(n_b,),
        in_specs=[pl.BlockSpec((ts, d), lambda b: (r0 + b, 0)),
                  pl.BlockSpec((ts, d), lambda b: (r0 + b, 1)),
                  pl.BlockSpec((ts, d), lambda b: (r0 + b, 2)),
                  cache_spec, cache_spec,
                  pl.BlockSpec((n_heads, UNIT, WIN), lambda b: (0, 0, 0))],
        out_specs=pl.BlockSpec((ts, d), lambda b: (b, 0)),
        out_shape=jax.ShapeDtypeStruct((n_b * ts, d), BF16),
        compiler_params=_params("arbitrary"),
        name="attn_sample",
    )(qkv, qkv, qkv, k_cache.reshape(cache_shape), v_cache.reshape(cache_shape), bias)


def _norm_blocked_kernel(x_ref, g_ref, o_ref, slab_scr):
    n_gb = o_ref.shape[0]
    tm = x_ref.shape[0]
    hn = _rms(x_ref[...], g_ref[...])
    for gb in range(n_gb):
        slab_scr[gb * tm:(gb + 1) * tm, :] = hn[:, gb * LANES:(gb + 1) * LANES]
    for gb in range(n_gb):
        for s in range(SSM_L):
            rows = pl.ds(gb * tm + s, tm // SSM_L, stride=SSM_L)
            o_ref[gb, :, s * LANES:(s + 1) * LANES] = slab_scr[rows, :].astype(BF16)


def _norm_blocked(x, gain):
    t, d = x.shape
    nc = t // SSM_L
    tm = _tile(t, 768, SSM_L * SUBLANES_BF16)
    n_gb = d // LANES
    return pl.pallas_call(
        _norm_blocked_kernel,
        grid=(t // tm,),
        in_specs=[pl.BlockSpec((tm, d), lambda i: (i, 0)),
                  pl.BlockSpec((1, d), lambda i: (0, 0))],
        out_specs=pl.BlockSpec((n_gb, tm // SSM_L, SSM_L * LANES), lambda i: (0, i, 0)),
        out_shape=jax.ShapeDtypeStruct((n_gb, nc, SSM_L * LANES), BF16),
        scratch_shapes=[pltpu.VMEM((n_gb * tm, LANES), F32)],
        compiler_params=_params("arbitrary"),
        name="ssm_norm",
    )(x, gain)


def _gelu_tanh(x):
    cdf = 0.5 * (1.0 + jnp.tanh(math.sqrt(2.0 / math.pi) * (x + 0.044715 * (x ** 3))))
    return x * cdf


def _ssm_core_kernel(u_ref, x0_ref, prow_ref, bmr_ref, bmi_ref, cmr_ref, cmi_ref, d_ref,
                     g_ref, xfp_ref, xfs_ref,
                     w_scr, cmt_scr, toep_scr, s_scr, xst_scr,
                     *, n_prompt, blocks_per_prompt, n_sample, blocks_per_sample, row_tile,
                     groups_per_tile):
    n_state = bmr_ref.shape[1]
    ch = cmr_ref.shape[0]

    def over_row_groups(ref):
        return jnp.concatenate([ref[...]] * (ch // ref.shape[0]), axis=0)

    def over_col_groups(ref):
        return jnp.concatenate([ref[...]] * (n_state // ref.shape[1]), axis=1)

    def discretise(a_re, a_im, log_dt):
        dt = jnp.exp(log_dt)
        mag = jnp.exp(dt * a_re)
        l_re = mag * jnp.cos(dt * a_im)
        l_im = mag * jnp.sin(dt * a_im)
        den = a_re * a_re + a_im * a_im
        n_re = l_re - 1.0
        n_im = l_im
        z_re = (n_re * a_re + n_im * a_im) / den
        z_im = (n_im * a_re - n_re * a_im) / den
        return l_re, l_im, z_re, z_im

    def powers(l_re, l_im, n):
        out = [(jnp.ones_like(l_re), jnp.zeros_like(l_im))]
        for _ in range(n):
            pr, pi = out[-1]
            out.append((pr * l_re - pi * l_im, pr * l_im + pi * l_re))
        return out

    lr, li, zr, zi = discretise(prow_ref[0:1, :], prow_ref[1:2, :], prow_ref[2:3, :])
    prow = powers(lr, li, SSM_L)

    p = n_state // groups_per_tile
    sg = ch // groups_per_tile
    mb = (lax.broadcasted_iota(jnp.int32, (ch, n_state), 0) // sg ==
          lax.broadcasted_iota(jnp.int32, (ch, n_state), 1) // p)

    bxr = over_row_groups(bmr_ref)
    bxi = over_row_groups(bmi_ref)
    bb_r = jnp.where(mb, zr * bxr - zi * bxi, 0.0)
    bb_i = jnp.where(mb, zr * bxi + zi * bxr, 0.0)

    for s in range(SSM_L):
        pr, pi = prow[SSM_L - 1 - s]
        rows = slice(s * ch, (s + 1) * ch)
        w_scr[rows, 0:n_state] = (pr * bb_r - pi * bb_i).astype(BF16)
        w_scr[rows, n_state:2 * n_state] = (pr * bb_i + pi * bb_r).astype(BF16)

    cxr = jnp.where(mb, over_col_groups(cmr_ref), 0.0)
    cxi = jnp.where(mb, over_col_groups(cmi_ref), 0.0)
    for tau in range(SSM_L + 1):
        pr, pi = prow[tau]
        rows = slice(tau * ch, (tau + 1) * ch)
        cmt_scr[rows, 0:n_state] = (cxr * pr - cxi * pi).astype(BF16)
        cmt_scr[rows, n_state:2 * n_state] = (-(cxr * pi + cxi * pr)).astype(BF16)

    width = SSM_L * ch
    bbar = jnp.concatenate([bb_r.astype(BF16), bb_i.astype(BF16)], axis=1)
    k_all = _dot_nt(bbar, cmt_scr[0:width, :]).astype(BF16)
    for s in range(SSM_L):
        rows = slice(s * ch, (s + 1) * ch)
        if s % 2 == 1:
            toep_scr[rows, (s - 1) * ch:s * ch] = jnp.zeros((ch, ch), BF16)
        toep_scr[rows, s * ch:width] = k_all[:, 0:width - s * ch]

    n_rows = u_ref.shape[0]
    for r in range(n_rows // row_tile):
        rows = slice(r * row_tile, (r + 1) * row_tile)
        s_scr[rows, :] = _dot(u_ref[rows, :], w_scr[...])

    l16r, l16i = prow[SSM_L]

    def step(xr, xi, sr, si):
        return l16r * xr - l16i * xi + sr, l16r * xi + l16i * xr + si

    def prompt_body(c, carry):
        new = []
        for b in range(n_prompt):
            xr, xi = carry[b]
            row = b * blocks_per_prompt + c
            xst_scr[pl.ds(row, 1), 0:n_state] = xr
            xst_scr[pl.ds(row, 1), n_state:2 * n_state] = xi
            new.append(step(xr, xi, s_scr[pl.ds(row, 1), 0:n_state],
                            s_scr[pl.ds(row, 1), n_state:2 * n_state]))
        return tuple(new)

    zero = jnp.zeros((1, n_state), F32)
    fin = lax.fori_loop(0, blocks_per_prompt, prompt_body, tuple((zero, zero) for _ in range(n_prompt)))
    for b in range(n_prompt):
        xfp_ref[b:b + 1, 0:n_state] = fin[b][0]
        xfp_ref[b:b + 1, n_state:2 * n_state] = fin[b][1]

    base = n_prompt * blocks_per_prompt
    for b in range(n_sample):
        xr = x0_ref[b:b + 1, 0:n_state]
        xi = x0_ref[b:b + 1, n_state:2 * n_state]
        for j in range(blocks_per_sample):
            row = base + b * blocks_per_sample + j
            xst_scr[row:row + 1, 0:n_state] = xr
            xst_scr[row:row + 1, n_state:2 * n_state] = xi
            xr, xi = step(xr, xi, s_scr[row:row + 1, 0:n_state], s_scr[row:row + 1, n_state:2 * n_state])
        xfs_ref[b:b + 1, 0:n_state] = xr
        xfs_ref[b:b + 1, n_state:2 * n_state] = xi

    dvec = d_ref[...]
    for r in range(n_rows // row_tile):
        rows = slice(r * row_tile, (r + 1) * row_tile)
        xst = xst_scr[rows, :].astype(BF16)
        for t in range(0, SSM_L, 2):
            k_ext = (t + 2) * ch
            y = (_dot(u_ref[rows, 0:k_ext], toep_scr[0:k_ext, t * ch:(t + 2) * ch])
                 + _dot_nt(xst, cmt_scr[(t + 1) * ch:(t + 3) * ch, :]))
            for tt in (t, t + 1):
                cols = slice(tt * ch, (tt + 1) * ch)
                yt = y[:, (tt - t) * ch:(tt - t + 1) * ch] + dvec * u_ref[rows, cols].astype(F32)
                g_ref[rows, cols] = _gelu_tanh(yt).astype(BF16)


def _ssm_core(u_blk, x0, prow, bmr, bmi, cmr, cmi, dvec, n_prompt, blocks_per_prompt, n_sample,
              blocks_per_sample, groups_per_tile):
    n_gb, nc, width = u_blk.shape
    ch = LANES
    sg, n_state = bmr.shape[1], bmr.shape[2]
    assert width == SSM_L * ch and nc == n_prompt * blocks_per_prompt + n_sample * blocks_per_sample
    row_tile = _tile(nc, 384)
    kern = functools.partial(_ssm_core_kernel, n_prompt=n_prompt, blocks_per_prompt=blocks_per_prompt,
                             n_sample=n_sample, blocks_per_sample=blocks_per_sample, row_tile=row_tile,
                             groups_per_tile=groups_per_tile)
    return pl.pallas_call(
        kern,
        grid=(n_gb,),
        in_specs=[pl.BlockSpec((None, nc, width), lambda g: (g, 0, 0)),
                  pl.BlockSpec((n_sample, 2 * n_state), lambda g: (0, g)),
                  pl.BlockSpec((None, 3, n_state), lambda g: (g, 0, 0)),
                  pl.BlockSpec((None, sg, n_state), lambda g: (g, 0, 0)),
                  pl.BlockSpec((None, sg, n_state), lambda g: (g, 0, 0)),
                  pl.BlockSpec((None, ch, LANES), lambda g: (g, 0, 0)),
                  pl.BlockSpec((None, ch, LANES), lambda g: (g, 0, 0)),
                  pl.BlockSpec((1, ch), lambda g: (0, g))],
        out_specs=[pl.BlockSpec((None, nc, width), lambda g: (g, 0, 0)),
                   pl.BlockSpec((n_prompt, 2 * n_state), lambda g: (0, g)),
                   pl.BlockSpec((n_sample, 2 * n_state), lambda g: (0, g))],
        out_shape=[jax.ShapeDtypeStruct((n_gb, nc, width), BF16),
                   jax.ShapeDtypeStruct((n_prompt, n_gb * 2 * n_state), F32),
                   jax.ShapeDtypeStruct((n_sample, n_gb * 2 * n_state), F32)],
        scratch_shapes=[pltpu.VMEM((width, 2 * n_state), BF16),
                        pltpu.VMEM((width + ch, 2 * n_state), BF16),
                        pltpu.VMEM((width, width), BF16),
                        pltpu.VMEM((nc, 2 * n_state), F32),
                        pltpu.VMEM((nc, 2 * n_state), F32)],
        compiler_params=_params("arbitrary"),
        name="ssm_core",
    )(u_blk, x0, prow, bmr, bmi, cmr, cmi, dvec)


def _glu_res_kernel(g_ref, x_ref, wa_ref, wb_ref, o_ref, slab_scr, lhs_scr):
    n_gb = g_ref.shape[0]
    tm = x_ref.shape[0]

    for gb in range(n_gb):
        for s in range(SSM_L):
            rows = pl.ds(gb * tm + s, tm // SSM_L, stride=SSM_L)
            slab_scr[rows, :] = g_ref[gb, :, s * LANES:(s + 1) * LANES].astype(F32)
    for gb in range(n_gb):
        lhs_scr[:, gb * LANES:(gb + 1) * LANES] = slab_scr[gb * tm:(gb + 1) * tm, :].astype(BF16)

    g = lhs_scr[...]
    a = _dot(g, wa_ref[...])
    b = _dot(g, wb_ref[...])
    o_ref[...] = x_ref[...] + a * jax.nn.sigmoid(b)


def _glu_res(x, g_blk, w_a_all, w_b_all, layer):
    t, d = x.shape
    n_gb, nc, _ = g_blk.shape
    tm = _tile(t, 512, SSM_L * SUBLANES_BF16)
    w_spec = pl.BlockSpec((None, d, d), lambda i: (layer, 0, 0), pipeline_mode=pl.Buffered(1))
    return pl.pallas_call(
        _glu_res_kernel,
        grid=(t // tm,),
        in_specs=[pl.BlockSpec((n_gb, tm // SSM_L, SSM_L * LANES), lambda i: (0, i, 0)),
                  pl.BlockSpec((tm, d), lambda i: (i, 0)),
                  w_spec, w_spec],
        out_specs=pl.BlockSpec((tm, d), lambda i: (i, 0)),
        out_shape=jax.ShapeDtypeStruct((t, d), F32),
        scratch_shapes=[pltpu.VMEM((n_gb * tm, LANES), F32),
                        pltpu.VMEM((tm, d), BF16)],
        compiler_params=_params("arbitrary"),
        name="glu_res",
    )(g_blk, x, w_a_all, w_b_all)


def _ssm_layouts(a_re, a_im, log_dt, b_re, b_im, c_re, c_im):
    g, p = a_re.shape
    sg = b_re.shape[2]
    gl = LANES // sg
    n_gb = g // gl
    dt = jnp.broadcast_to(log_dt[:, None], (g, p))
    prow = jnp.stack([a_re, a_im, dt], axis=0).reshape(3, n_gb, gl * p).transpose(1, 0, 2)

    def b_rows(b):
        return b.reshape(n_gb, gl, p, sg).transpose(0, 3, 1, 2).reshape(n_gb, sg, gl * p)

    def c_cols(c):
        return jnp.tile(c, (1, 1, LANES // p)).reshape(n_gb, gl * sg, LANES)

    return prow, b_rows(b_re), b_rows(b_im), c_cols(c_re), c_cols(c_im)


def _state_to_tiles(st_re, st_im, gl):
    b, g, p = st_re.shape
    n_gb = g // gl
    st = jnp.stack([st_re.reshape(b, n_gb, gl * p), st_im.reshape(b, n_gb, gl * p)], axis=2)
    return st.reshape(b, n_gb * 2 * gl * p)


def _tiles_to_state(x, g, p, gl):
    b = x.shape[0]
    n_gb = g // gl
    x = x.reshape(b, n_gb, 2, gl, p)
    return x[:, :, 0].reshape(b, g, p), x[:, :, 1].reshape(b, g, p)


def kernel(x_prompt, x_sample, cache_attn_k, cache_attn_v, state_ssm_re, state_ssm_im, norm_mix, norm_mlp, norm_final, attn_w_qkv, attn_w_o, attn_rel_bias, ssm_a_re, ssm_a_im, ssm_log_dt, ssm_b_re, ssm_b_im, ssm_c_re, ssm_c_im, ssm_d, ssm_w_glu_a, ssm_w_glu_b, mlp_w_up, mlp_w_down):
    bp, seq, d = x_prompt.shape
    bs, ts, _ = x_sample.shape
    depth = norm_mix.shape[0]
    n_heads, dh = cache_attn_k.shape[3], cache_attn_k.shape[4]
    n_groups, p_state = ssm_a_re.shape[1], ssm_a_re.shape[2]
    sg = d // n_groups
    gl = LANES // sg
    assert ts == CHUNK and seq % KV_TAIL == 0 and (bs * ts) % KV_TAIL == 0
    assert seq % SSM_L == 0 and ts % SSM_L == 0 and d % LANES == 0

    n_prompt_rows = bp * seq
    xs = (x_prompt.reshape(n_prompt_rows, d), x_sample.reshape(bs * ts, d))

    w_qkv = attn_w_qkv.astype(BF16)
    w_o = attn_w_o.astype(BF16)
    w_glu_a = ssm_w_glu_a.astype(BF16)
    w_glu_b = ssm_w_glu_b.astype(BF16)

    k_p, v_p, k_s, v_s = [], [], [], []
    sr_p, si_p, sr_s, si_s = [], [], [], []
    for i in range(depth):
        j = i // 2
        gain = norm_mix[i][None, :]
        if i % 2 == 0:
            qkv, k_tail, v_tail = _qkv(xs, gain, w_qkv, j, seq // KV_TAIL, bp, dh ** -0.5 * LOG2_E)
            bias = _bias_table(attn_rel_bias[j])
            o_p = _attn_prompt(qkv, bias, bp, seq, d, dh)
            o_s = _attn_sample(qkv, cache_attn_k, cache_attn_v, j, bias, n_prompt_rows, d, dh)
            x = _proj_res(xs, o_p, o_s, w_o, j)
            n_p, n_s = bp * KV_TAIL, bs * ts
            k_p.append(k_tail[:n_p].reshape(bp, KV_TAIL, n_heads, dh))
            v_p.append(v_tail[:n_p].reshape(bp, KV_TAIL, n_heads, dh))
            k_s.append(k_tail[n_p:n_p + n_s].reshape(bs, ts, n_heads, dh))
            v_s.append(v_tail[n_p:n_p + n_s].reshape(bs, ts, n_heads, dh))
        else:
            x = xs[0]
            layouts = _ssm_layouts(ssm_a_re[j], ssm_a_im[j], ssm_log_dt[j], ssm_b_re[j], ssm_b_im[j],
                                   ssm_c_re[j], ssm_c_im[j])
            u_blk = _norm_blocked(x, gain)
            x0 = _state_to_tiles(state_ssm_re[j], state_ssm_im[j], gl)
            g_blk, xf_p, xf_s = _ssm_core(u_blk, x0, *layouts, ssm_d[j][None, :], bp, seq // SSM_L, bs,
                                          ts // SSM_L, gl)
            x = _glu_res(x, g_blk, w_glu_a, w_glu_b, j)
            rp, ip = _tiles_to_state(xf_p, n_groups, p_state, gl)
            rs, is_ = _tiles_to_state(xf_s, n_groups, p_state, gl)
            sr_p.append(rp); si_p.append(ip); sr_s.append(rs); si_s.append(is_)
        x = _mlp(x, norm_mlp[i][None, :], mlp_w_up, mlp_w_down, i)
        xs = (x,)

    y_p, y_s = _final_norm(xs[0], norm_final[None, :], n_prompt_rows)
    return (y_p.reshape(bp, seq, d), y_s.reshape(bs, ts, d),
            jnp.stack(k_p), jnp.stack(v_p), jnp.stack(sr_p), jnp.stack(si_p),
            jnp.stack(k_s), jnp.stack(v_s), jnp.stack(sr_s), jnp.stack(si_s))
```

```python
import functools
import math

import jax
import jax.numpy as jnp
from jax import lax
from jax.experimental import pallas as pl
from jax.experimental.pallas import tpu as pltpu

CHUNK = 64
N_PAST_CHUNKS = 8
RMS_EPS = 1e-5
NEG_INF = -1e30
LOG2_E = math.log2(math.e)

LANES = 128
SUBLANES_BF16 = 16
VMEM_LIMIT_BYTES = 58 * 1024 * 1024

PAIR = 2 * CHUNK
UNIT = 2 * PAIR
KV_TAIL = CHUNK * N_PAST_CHUNKS
WIN = KV_TAIL + UNIT
SSM_L = 16

F32 = jnp.float32
BF16 = jnp.bfloat16


def _params(*sem):
    return pltpu.CompilerParams(dimension_semantics=sem, vmem_limit_bytes=VMEM_LIMIT_BYTES)


def _tile(n, target, mult=SUBLANES_BF16):
    best = None
    for t in range(mult, min(n, target) + 1, mult):
        if n % t == 0:
            best = t
    assert best is not None, (n, target, mult)
    return best


def _rms(x, g):
    y = x * lax.rsqrt(jnp.mean(x * x, axis=-1, keepdims=True) + RMS_EPS)
    return y * g


def _dot(a, b):
    return jnp.dot(a, b, preferred_element_type=F32)


def _dot_nt(a, b):
    return lax.dot_general(a, b, (((1,), (1,)), ((), ())), preferred_element_type=F32)


def _x_specs(xs, tm, d, n_prompt_tiles, grid_rank, row_axis):
    def at(f):
        return lambda *idx: (f(idx[row_axis]), 0)
    if len(xs) == 1:
        return [pl.BlockSpec((tm, d), at(lambda i: i))]
    return [pl.BlockSpec((tm, d), at(lambda i: jnp.minimum(i, n_prompt_tiles - 1))),
            pl.BlockSpec((tm, d), at(lambda i: jnp.maximum(i - n_prompt_tiles, 0)))]


def _read_x(x_refs, i, n_prompt_tiles):
    if len(x_refs) == 1:
        return x_refs[0][...]
    return jnp.where(i < n_prompt_tiles, x_refs[0][...], x_refs[1][...])


def _qkv_kernel(*refs, n_src, n_prompt_tiles, q_scale):
    x_refs, (g_ref, w_ref, o_ref, tail_ref) = refs[:n_src], refs[n_src:]
    x = _read_x(x_refs, pl.program_id(1), n_prompt_tiles)
    hn = _rms(x, g_ref[...]).astype(BF16)
    acc = _dot(hn, w_ref[...])
    scale = jnp.where(pl.program_id(0) == 0, q_scale, 1.0).astype(F32)
    o_ref[...] = (acc * scale).astype(BF16)
    tail_ref[...] = acc


def _qkv(xs, gain, w_all, layer, n_prompt_tiles_per_batch, n_prompt_batches, q_scale):
    d = xs[0].shape[1]
    t = sum(x.shape[0] for x in xs)
    n = w_all.shape[2]
    tm = KV_TAIL
    tn = d
    assert n == 3 * tn
    nt = t // tm
    n_prompt_tiles = n_prompt_tiles_per_batch * n_prompt_batches
    n_slots = n_prompt_batches + (nt - n_prompt_tiles)

    def slot(i):
        return jnp.where(i < n_prompt_tiles, i // n_prompt_tiles_per_batch,
                         i - n_prompt_tiles + n_prompt_batches)

    def tail_block(j, i):
        return jnp.where(j == 0, 2 * n_slots, (j - 1) * n_slots + slot(i)), 0

    return pl.pallas_call(
        functools.partial(_qkv_kernel, n_src=len(xs), n_prompt_tiles=n_prompt_tiles, q_scale=q_scale),
        grid=(n // tn, nt),
        in_specs=_x_specs(xs, tm, d, n_prompt_tiles, 2, 1) + [
            pl.BlockSpec((1, d), lambda j, i: (0, 0)),
            pl.BlockSpec((None, d, tn), lambda j, i: (layer, 0, j))],
        out_specs=[pl.BlockSpec((tm, tn), lambda j, i: (i, j)),
                   pl.BlockSpec((tm, tn), tail_block)],
        out_shape=[jax.ShapeDtypeStruct((t, n), BF16),
                   jax.ShapeDtypeStruct(((2 * n_slots + 1) * tm, tn), F32)],
        compiler_params=_params("arbitrary", "arbitrary"),
        name="qkv",
    )(*xs, gain, w_all)


def _proj_res_kernel(*refs, n_src, n_prompt_tiles):
    x_refs, a_refs, (w_ref, o_ref) = refs[:n_src], refs[n_src:n_src + 2], refs[n_src + 2:]
    i = pl.program_id(0)
    o_ref[...] = _read_x(x_refs, i, n_prompt_tiles) + _dot(_read_x(a_refs, i, n_prompt_tiles), w_ref[...])


def _proj_res(xs, a_prompt, a_sample, w_all, layer):
    d = xs[0].shape[1]
    n_prompt_rows, k = a_prompt.shape
    t = n_prompt_rows + a_sample.shape[0]
    tm = _tile(math.gcd(n_prompt_rows, t - n_prompt_rows), 512)
    npt = n_prompt_rows // tm
    return pl.pallas_call(
        functools.partial(_proj_res_kernel, n_src=len(xs), n_prompt_tiles=npt),
        grid=(t // tm,),
        in_specs=_x_specs(xs, tm, d, npt, 1, 0) + _x_specs((a_prompt, a_sample), tm, k, npt, 1, 0) + [
            pl.BlockSpec((None, k, d), lambda i: (layer, 0, 0), pipeline_mode=pl.Buffered(1))],
        out_specs=pl.BlockSpec((tm, d), lambda i: (i, 0)),
        out_shape=jax.ShapeDtypeStruct((t, d), F32),
        compiler_params=_params("arbitrary"),
        name="proj_res",
    )(*xs, a_prompt, a_sample, w_all)


def _mlp_kernel(x_ref, g_ref, wu_ref, wd_ref, o_ref, hn_ref):
    k = pl.program_id(1)

    @pl.when(k == 0)
    def _():
        x = x_ref[...]
        hn_ref[...] = _rms(x, g_ref[...]).astype(BF16)
        o_ref[...] = x

    h = jnp.maximum(_dot(hn_ref[...], wu_ref[...]), 0.0)
    o_ref[...] += _dot((h * h).astype(BF16), wd_ref[...])


def _mlp_first_kernel(x_ref, g_ref, wu32_ref, wd32_ref, o_ref, wu_ref, wd_ref, hn_ref):
    k = pl.program_id(0)

    @pl.when(k == 0)
    def _():
        x = x_ref[...]
        hn_ref[...] = _rms(x, g_ref[...]).astype(BF16)
        o_ref[...] = x

    wu = wu32_ref[...].astype(BF16)
    wd = wd32_ref[...].astype(BF16)
    wu_ref[...] = wu
    wd_ref[...] = wd
    h = jnp.maximum(_dot(hn_ref[...], wu), 0.0)
    o_ref[...] += _dot((h * h).astype(BF16), wd)


def _mlp_rest_kernel(x_ref, g_ref, wu_ref, wd_ref, first_ref, o_ref, hn_ref):
    i = pl.program_id(0)

    @pl.when((i == 0) & (pl.program_id(1) == 0))
    def _():
        o_ref[...] = first_ref[...]

    @pl.when(i > 0)
    def _():
        _mlp_kernel(x_ref, g_ref, wu_ref, wd_ref, o_ref, hn_ref)


def _mlp(x, gain, w_up_f32, w_down_f32, layer):
    t, d = x.shape
    dff = w_up_f32.shape[2]
    tm = _tile(t, 768)
    tf1 = _tile(dff, 512, LANES)
    o_first, w_up, w_down = pl.pallas_call(
        _mlp_first_kernel,
        grid=(dff // tf1,),
        in_specs=[pl.BlockSpec((tm, d), lambda k: (0, 0)),
                  pl.BlockSpec((1, d), lambda k: (0, 0)),
                  pl.BlockSpec((None, d, tf1), lambda k: (layer, 0, k)),
                  pl.BlockSpec((None, tf1, d), lambda k: (layer, k, 0))],
        out_specs=[pl.BlockSpec((tm, d), lambda k: (0, 0)),
                   pl.BlockSpec((d, tf1), lambda k: (0, k)),
                   pl.BlockSpec((tf1, d), lambda k: (k, 0))],
        out_shape=[jax.ShapeDtypeStruct((tm, d), F32),
                   jax.ShapeDtypeStruct((d, dff), BF16),
                   jax.ShapeDtypeStruct((dff, d), BF16)],
        scratch_shapes=[pltpu.VMEM((tm, d), BF16)],
        compiler_params=_params("arbitrary"),
        name="mlp_first",
    )(x, gain, w_up_f32, w_down_f32)
    tf = _tile(dff, 1024, LANES)

    def chunk(i, k):
        return jnp.where(i == 0, 0, k)

    return pl.pallas_call(
        _mlp_rest_kernel,
        grid=(t // tm, dff // tf),
        in_specs=[pl.BlockSpec((tm, d), lambda i, k: (i, 0)),
                  pl.BlockSpec((1, d), lambda i, k: (0, 0)),
                  pl.BlockSpec((d, tf), lambda i, k: (0, chunk(i, k))),
                  pl.BlockSpec((tf, d), lambda i, k: (chunk(i, k), 0)),
                  pl.BlockSpec((tm, d), lambda i, k: (0, 0), pipeline_mode=pl.Buffered(1))],
        out_specs=pl.BlockSpec((tm, d), lambda i, k: (i, 0)),
        out_shape=jax.ShapeDtypeStruct((t, d), F32),
        scratch_shapes=[pltpu.VMEM((tm, d), BF16)],
        compiler_params=_params("arbitrary", "arbitrary"),
        name="mlp",
    )(x, gain, w_up, w_down, o_first)


def _final_norm_kernel(x_ref, g_ref, yp_ref, ys_ref, *, n_prompt_tiles):
    i = pl.program_id(0)
    y = _rms(x_ref[...], g_ref[...])

    @pl.when(i < n_prompt_tiles)
    def _():
        yp_ref[...] = y

    @pl.when(i >= n_prompt_tiles)
    def _():
        ys_ref[...] = y


def _final_norm(x, gain, n_prompt_rows):
    t, d = x.shape
    n_sample_rows = t - n_prompt_rows
    tm = _tile(math.gcd(n_prompt_rows, n_sample_rows), 512)
    npt = n_prompt_rows // tm
    return pl.pallas_call(
        functools.partial(_final_norm_kernel, n_prompt_tiles=npt),
        grid=(t // tm,),
        in_specs=[pl.BlockSpec((tm, d), lambda i: (i, 0)),
                  pl.BlockSpec((1, d), lambda i: (0, 0))],
        out_specs=[pl.BlockSpec((tm, d), lambda i: (jnp.minimum(i, npt - 1), 0)),
                   pl.BlockSpec((tm, d), lambda i: (jnp.maximum(i - npt, 0), 0))],
        out_shape=[jax.ShapeDtypeStruct((n_prompt_rows, d), F32),
                   jax.ShapeDtypeStruct((n_sample_rows, d), F32)],
        compiler_params=_params("arbitrary"),
        name="final_norm",
    )(x, gain)


def _bias_kernel(t_ref, o_ref):
    n_heads = o_ref.shape[0]
    width = t_ref.shape[1]
    r = lax.broadcasted_iota(jnp.int32, (UNIT, WIN), 0)
    c = lax.broadcasted_iota(jnp.int32, (UNIT, WIN), 1)
    dist = r // PAIR + KV_TAIL // PAIR - c // PAIR
    q_hi = (r % PAIR) >= CHUNK
    k_hi = (c % PAIR) >= CHUNK
    masked = ((dist < 0) | (dist > KV_TAIL // PAIR)
              | ((dist == 0) & jnp.logical_not(q_hi) & k_hi)
              | ((dist == KV_TAIL // PAIR) & q_hi & jnp.logical_not(k_hi)))
    for h in range(n_heads):
        rows = jnp.broadcast_to(t_ref[h:h + 1, :], (UNIT, width))
        rolled = pltpu.roll(rows, width - (UNIT - 1), 1, stride=1, stride_axis=0)
        o_ref[h] = jnp.where(masked, NEG_INF, rolled[:, :WIN] * LOG2_E)


def _bias_table(rel_bias):
    n_heads, n_rel = rel_bias.shape
    max_rel = (n_rel - 1) // 2
    assert max_rel == PAIR
    width = WIN + UNIT
    n_far = WIN - 1 - max_rel + 1
    tvec = jnp.concatenate(
        [jnp.broadcast_to(rel_bias[:, 2 * max_rel:], (n_heads, n_far)),
         jnp.flip(rel_bias[:, :2 * max_rel], axis=1),
         jnp.broadcast_to(rel_bias[:, :1], (n_heads, width - n_far - 2 * max_rel))], axis=1)
    return pl.pallas_call(
        _bias_kernel,
        out_shape=jax.ShapeDtypeStruct((n_heads, UNIT, WIN), F32),
        compiler_params=pltpu.CompilerParams(vmem_limit_bytes=VMEM_LIMIT_BYTES),
        name="attn_bias",
    )(tvec)


def _softmax_pv(scores, values):
    dh = values[0].shape[1]
    m = scores[0].max(axis=1, keepdims=True)
    for s in scores[1:]:
        m = jnp.maximum(m, s.max(axis=1, keepdims=True))
    acc = None
    for s, v in zip(scores, values):
        p = jnp.exp2(s - m).astype(BF16)
        r = _dot(p, jnp.concatenate([v, jnp.ones_like(v)], axis=1))
        acc = r if acc is None else acc + r
    return acc[:, :dh] / acc[:, dh:]


def _attn_prompt_kernel(q_ref, k_ref, v_ref, b_ref, o_ref, *, n_heads, dh, n_units):
    qi = pl.program_id(2)

    def run(items):
        def raw_scores(item):
            q_rows, h, key_rows, _, _ = item
            hs = slice(h * dh, (h + 1) * dh)
            return _dot_nt(q_ref[q_rows, hs], k_ref[key_rows, hs])

        s_next = raw_scores(items[0])
        for n, item in enumerate(items):
            s_raw = s_next
            if n + 1 < len(items):
                s_next = raw_scores(items[n + 1])
            q_rows, h, key_rows, _, b_cols = item
            hs = slice(h * dh, (h + 1) * dh)
            o = _softmax_pv([s_raw + b_ref[h, :, b_cols]], [v_ref[key_rows, hs]])
            o_ref[q_rows, hs] = o.astype(BF16)

    @pl.when(qi == 0)
    def _():
        items = []
        for u in range(n_units):
            nk = (u + 1) * UNIT
            for h in range(n_heads):
                items.append((slice(u * UNIT, (u + 1) * UNIT), h, slice(0, nk), nk, slice(WIN - nk, WIN)))
        run(items)

    @pl.when(qi > 0)
    def _():
        items = []
        for u in range(n_units):
            k0 = pl.multiple_of((qi * n_units + u) * UNIT - KV_TAIL, PAIR)
            for h in range(n_heads):
                items.append((slice(u * UNIT, (u + 1) * UNIT), h, pl.ds(k0, WIN), WIN, slice(0, WIN)))
        run(items)


def _attn_prompt(qkv, bias, n_batches, seq, d, dh):
    n_units = KV_TAIL // UNIT
    tq = n_units * UNIT
    assert seq % tq == 0 and seq >= WIN
    hw = min(d, 4 * dh)
    n_hg = d // hw
    n_heads = hw // dh
    return pl.pallas_call(
        functools.partial(_attn_prompt_kernel, n_heads=n_heads, dh=dh, n_units=n_units),
        grid=(n_batches, n_hg, seq // tq),
        in_specs=[pl.BlockSpec((tq, hw), lambda b, g, i: (b * (seq // tq) + i, g)),
                  pl.BlockSpec((seq, hw), lambda b, g, i: (b, n_hg + g)),
                  pl.BlockSpec((seq, hw), lambda b, g, i: (b, 2 * n_hg + g)),
                  pl.BlockSpec((n_heads, UNIT, WIN), lambda b, g, i: (g, 0, 0))],
        out_specs=pl.BlockSpec((tq, hw), lambda b, g, i: (b * (seq // tq) + i, g)),
        out_shape=jax.ShapeDtypeStruct((n_batches * seq, d), BF16),
        compiler_params=_params("arbitrary", "arbitrary", "arbitrary"),
        name="attn_prompt",
    )(qkv, qkv, qkv, bias)


def _attn_sample_kernel(q_ref, kn_ref, vn_ref, kc_ref, vc_ref, b_ref, o_ref, *, n_heads, dh):
    ts = q_ref.shape[0]
    lc = kc_ref.shape[0] // n_heads

    def cached(ref, h):
        return ref[pl.ds(h, lc, stride=n_heads), :].astype(BF16)

    def raw_scores(h):
        hs = slice(h * dh, (h + 1) * dh)
        q = q_ref[:, hs]
        return _dot_nt(q, cached(kc_ref, h)), _dot_nt(q, kn_ref[:, hs])

    s_next = raw_scores(0)
    for h in range(n_heads):
        s_c, s_n = s_next
        if h + 1 < n_heads:
            s_next = raw_scores(h + 1)
        hs = slice(h * dh, (h + 1) * dh)
        o = _softmax_pv([s_c + b_ref[h, :ts, :lc], s_n + b_ref[h, :ts, lc:lc + ts]],
                        [cached(vc_ref, h), vn_ref[:, hs]])
        o_ref[:, hs] = o.astype(BF16)


def _attn_sample(qkv, k_cache, v_cache, layer, bias, row0, d, dh):
    _, n_b, lc, n_heads, _ = k_cache.shape
    ts = CHUNK
    assert lc == KV_TAIL and row0 % ts == 0 and n_heads * dh == d
    r0 = row0 // ts
    cache_shape = (k_cache.shape[0], n_b, lc * n_heads, dh)
    cache_spec = pl.BlockSpec((None, None, lc * n_heads, dh), lambda b: (layer, b, 0, 0))
    return pl.pallas_call(
        functools.partial(_attn_sample_kernel, n_heads=n_heads, dh=dh),
        grid=(n_b,),
        in_specs=[pl.BlockSpec((ts, d), lambda b: (r0 + b, 0)),
                  pl.BlockSpec((ts, d), lambda b: (r0 + b, 1)),
                  pl.BlockSpec((ts, d), lambda b: (r0 + b, 2)),
                  cache_spec, cache_spec,
                  pl.BlockSpec((n_heads, UNIT, WIN), lambda b: (0, 0, 0))],
        out_specs=pl.BlockSpec((ts, d), lambda b: (b, 0)),
        out_shape=jax.ShapeDtypeStruct((n_b * ts, d), BF16),
        compiler_params=_params("arbitrary"),
        name="attn_sample",
    )(qkv, qkv, qkv, k_cache.reshape(cache_shape), v_cache.reshape(cache_shape), bias)


def _norm_blocked_kernel(x_ref, g_ref, o_ref, slab_scr):
    n_gb = o_ref.shape[0]
    tm = x_ref.shape[0]
    hn = _rms(x_ref[...], g_ref[...])
    for gb in range(n_gb):
        slab_scr[gb * tm:(gb + 1) * tm, :] = hn[:, gb * LANES:(gb + 1) * LANES]
    for gb in range(n_gb):
        for s in range(SSM_L):
            rows = pl.ds(gb * tm + s, tm // SSM_L, stride=SSM_L)
            o_ref[gb, :, s * LANES:(s + 1) * LANES] = slab_scr[rows, :].astype(BF16)


def _norm_blocked(x, gain):
    t, d = x.shape
    nc = t // SSM_L
    tm = _tile(t, 768, SSM_L * SUBLANES_BF16)
    n_gb = d // LANES
    return pl.pallas_call(
        _norm_blocked_kernel,
        grid=(t // tm,),
        in_specs=[pl.BlockSpec((tm, d), lambda i: (i, 0)),
                  pl.BlockSpec((1, d), lambda i: (0, 0))],
        out_specs=pl.BlockSpec((n_gb, tm // SSM_L, SSM_L * LANES), lambda i: (0, i, 0)),
        out_shape=jax.ShapeDtypeStruct((n_gb, nc, SSM_L * LANES), BF16),
        scratch_shapes=[pltpu.VMEM((n_gb * tm, LANES), F32)],
        compiler_params=_params("arbitrary"),
        name="ssm_norm",
    )(x, gain)


def _gelu_tanh(x):
    cdf = 0.5 * (1.0 + jnp.tanh(math.sqrt(2.0 / math.pi) * (x + 0.044715 * (x ** 3))))
    return x * cdf


def _ssm_core_kernel(u_ref, x0_ref, prow_ref, bmr_ref, bmi_ref, cmr_ref, cmi_ref, d_ref,
                     g_ref, xfp_ref, xfs_ref,
                     w_scr, cmt_scr, toep_scr, s_scr, xst_scr,
                     *, n_prompt, blocks_per_prompt, n_sample, blocks_per_sample, row_tile,
                     groups_per_tile):
    n_state = bmr_ref.shape[1]
    ch = cmr_ref.shape[0]

    def over_row_groups(ref):
        return jnp.concatenate([ref[...]] * (ch // ref.shape[0]), axis=0)

    def over_col_groups(ref):
        return jnp.concatenate([ref[...]] * (n_state // ref.shape[1]), axis=1)

    def discretise(a_re, a_im, log_dt):
        dt = jnp.exp(log_dt)
        mag = jnp.exp(dt * a_re)
        l_re = mag * jnp.cos(dt * a_im)
        l_im = mag * jnp.sin(dt * a_im)
        den = a_re * a_re + a_im * a_im
        n_re = l_re - 1.0
        n_im = l_im
        z_re = (n_re * a_re + n_im * a_im) / den
        z_im = (n_im * a_re - n_re * a_im) / den
        return l_re, l_im, z_re, z_im

    def powers(l_re, l_im, n):
        out = [(jnp.ones_like(l_re), jnp.zeros_like(l_im))]
        for _ in range(n):
            pr, pi = out[-1]
            out.append((pr * l_re - pi * l_im, pr * l_im + pi * l_re))
        return out

    lr, li, zr, zi = discretise(prow_ref[0:1, :], prow_ref[1:2, :], prow_ref[2:3, :])
    prow = powers(lr, li, SSM_L)

    p = n_state // groups_per_tile
    sg = ch // groups_per_tile
    mb = (lax.broadcasted_iota(jnp.int32, (ch, n_state), 0) // sg ==
          lax.broadcasted_iota(jnp.int32, (ch, n_state), 1) // p)

    bxr = over_row_groups(bmr_ref)
    bxi = over_row_groups(bmi_ref)
    bb_r = jnp.where(mb, zr * bxr - zi * bxi, 0.0)
    bb_i = jnp.where(mb, zr * bxi + zi * bxr, 0.0)

    for s in range(SSM_L):
        pr, pi = prow[SSM_L - 1 - s]
        rows = slice(s * ch, (s + 1) * ch)
        w_scr[rows, 0:n_state] = (pr * bb_r - pi * bb_i).astype(BF16)
        w_scr[rows, n_state:2 * n_state] = (pr * bb_i + pi * bb_r).astype(BF16)

    cxr = jnp.where(mb, over_col_groups(cmr_ref), 0.0)
    cxi = jnp.where(mb, over_col_groups(cmi_ref), 0.0)
    for tau in range(SSM_L + 1):
        pr, pi = prow[tau]
        rows = slice(tau * ch, (tau + 1) * ch)
        cmt_scr[rows, 0:n_state] = (cxr * pr - cxi * pi).astype(BF16)
        cmt_scr[rows, n_state:2 * n_state] = (-(cxr * pi + cxi * pr)).astype(BF16)

    width = SSM_L * ch
    bbar = jnp.concatenate([bb_r.astype(BF16), bb_i.astype(BF16)], axis=1)
    k_all = _dot_nt(bbar, cmt_scr[0:width, :]).astype(BF16)
    for s in range(SSM_L):
        rows = slice(s * ch, (s + 1) * ch)
        if s % 2 == 1:
            toep_scr[rows, (s - 1) * ch:s * ch] = jnp.zeros((ch, ch), BF16)
        toep_scr[rows, s * ch:width] = k_all[:, 0:width - s * ch]

    n_rows = u_ref.shape[0]
    for r in range(n_rows // row_tile):
        rows = slice(r * row_tile, (r + 1) * row_tile)
        s_scr[rows, :] = _dot(u_ref[rows, :], w_scr[...])

    l16r, l16i = prow[SSM_L]

    def step(xr, xi, sr, si):
        return l16r * xr - l16i * xi + sr, l16r * xi + l16i * xr + si

    def prompt_body(c, carry):
        new = []
        for b in range(n_prompt):
            xr, xi = carry[b]
            row = b * blocks_per_prompt + c
            xst_scr[pl.ds(row, 1), 0:n_state] = xr
            xst_scr[pl.ds(row, 1), n_state:2 * n_state] = xi
            new.append(step(xr, xi, s_scr[pl.ds(row, 1), 0:n_state],
                            s_scr[pl.ds(row, 1), n_state:2 * n_state]))
        return tuple(new)

    zero = jnp.zeros((1, n_state), F32)
    fin = lax.fori_loop(0, blocks_per_prompt, prompt_body, tuple((zero, zero) for _ in range(n_prompt)))
    for b in range(n_prompt):
        xfp_ref[b:b + 1, 0:n_state] = fin[b][0]
        xfp_ref[b:b + 1, n_state:2 * n_state] = fin[b][1]

    base = n_prompt * blocks_per_prompt
    for b in range(n_sample):
        xr = x0_ref[b:b + 1, 0:n_state]
        xi = x0_ref[b:b + 1, n_state:2 * n_state]
        for j in range(blocks_per_sample):
            row = base + b * blocks_per_sample + j
            xst_scr[row:row + 1, 0:n_state] = xr
            xst_scr[row:row + 1, n_state:2 * n_state] = xi
            xr, xi = step(xr, xi, s_scr[row:row + 1, 0:n_state], s_scr[row:row + 1, n_state:2 * n_state])
        xfs_ref[b:b + 1, 0:n_state] = xr
        xfs_ref[b:b + 1, n_state:2 * n_state] = xi

    dvec = d_ref[...]
    for r in range(n_rows // row_tile):
        rows = slice(r * row_tile, (r + 1) * row_tile)
        xst = xst_scr[rows, :].astype(BF16)
        for t in range(0, SSM_L, 2):
            k_ext = (t + 2) * ch
            y = (_dot(u_ref[rows, 0:k_ext], toep_scr[0:k_ext, t * ch:(t + 2) * ch])
                 + _dot_nt(xst, cmt_scr[(t + 1) * ch:(t + 3) * ch, :]))
            for tt in (t, t + 1):
                cols = slice(tt * ch, (tt + 1) * ch)
                yt = y[:, (tt - t) * ch:(tt - t + 1) * ch] + dvec * u_ref[rows, cols].astype(F32)
                g_ref[rows, cols] = _gelu_tanh(yt).astype(BF16)


def _ssm_core(u_blk, x0, prow, bmr, bmi, cmr, cmi, dvec, n_prompt, blocks_per_prompt, n_sample,
              blocks_per_sample, groups_per_tile):
    n_gb, nc, width = u_blk.shape
    ch = LANES
    sg, n_state = bmr.shape[1], bmr.shape[2]
    assert width == SSM_L * ch and nc == n_prompt * blocks_per_prompt + n_sample * blocks_per_sample
    row_tile = _tile(nc, 384)
    kern = functools.partial(_ssm_core_kernel, n_prompt=n_prompt, blocks_per_prompt=blocks_per_prompt,
                             n_sample=n_sample, blocks_per_sample=blocks_per_sample, row_tile=row_tile,
                             groups_per_tile=groups_per_tile)
    return pl.pallas_call(
        kern,
        grid=(n_gb,),
        in_specs=[pl.BlockSpec((None, nc, width), lambda g: (g, 0, 0)),
                  pl.BlockSpec((n_sample, 2 * n_state), lambda g: (0, g)),
                  pl.BlockSpec((None, 3, n_state), lambda g: (g, 0, 0)),
                  pl.BlockSpec((None, sg, n_state), lambda g: (g, 0, 0)),
                  pl.BlockSpec((None, sg, n_state), lambda g: (g, 0, 0)),
                  pl.BlockSpec((None, ch, LANES), lambda g: (g, 0, 0)),
                  pl.BlockSpec((None, ch, LANES), lambda g: (g, 0, 0)),
                  pl.BlockSpec((1, ch), lambda g: (0, g))],
        out_specs=[pl.BlockSpec((None, nc, width), lambda g: (g, 0, 0)),
                   pl.BlockSpec((n_prompt, 2 * n_state), lambda g: (0, g)),
                   pl.BlockSpec((n_sample, 2 * n_state), lambda g: (0, g))],
        out_shape=[jax.ShapeDtypeStruct((n_gb, nc, width), BF16),
                   jax.ShapeDtypeStruct((n_prompt, n_gb * 2 * n_state), F32),
                   jax.ShapeDtypeStruct((n_sample, n_gb * 2 * n_state), F32)],
        scratch_shapes=[pltpu.VMEM((width, 2 * n_state), BF16),
                        pltpu.VMEM((width + ch, 2 * n_state), BF16),
                        pltpu.VMEM((width, width), BF16),
                        pltpu.VMEM((nc, 2 * n_state), F32),
                        pltpu.VMEM((nc, 2 * n_state), F32)],
        compiler_params=_params("arbitrary"),
        name="ssm_core",
    )(u_blk, x0, prow, bmr, bmi, cmr, cmi, dvec)


def _glu_res_kernel(g_ref, x_ref, wa_ref, wb_ref, o_ref, slab_scr, lhs_scr):
    n_gb = g_ref.shape[0]
    tm = x_ref.shape[0]

    for gb in range(n_gb):
        for s in range(SSM_L):
            rows = pl.ds(gb * tm + s, tm // SSM_L, stride=SSM_L)
            slab_scr[rows, :] = g_ref[gb, :, s * LANES:(s + 1) * LANES].astype(F32)
    for gb in range(n_gb):
        lhs_scr[:, gb * LANES:(gb + 1) * LANES] = slab_scr[gb * tm:(gb + 1) * tm, :].astype(BF16)

    g = lhs_scr[...]
    a = _dot(g, wa_ref[...])
    b = _dot(g, wb_ref[...])
    o_ref[...] = x_ref[...] + a * jax.nn.sigmoid(b)


def _glu_res(x, g_blk, w_a_all, w_b_all, layer):
    t, d = x.shape
    n_gb, nc, _ = g_blk.shape
    tm = _tile(t, 512, SSM_L * SUBLANES_BF16)
    w_spec = pl.BlockSpec((None, d, d), lambda i: (layer, 0, 0), pipeline_mode=pl.Buffered(1))
    return pl.pallas_call(
        _glu_res_kernel,
        grid=(t // tm,),
        in_specs=[pl.BlockSpec((n_gb, tm // SSM_L, SSM_L * LANES), lambda i: (0, i, 0)),
                  pl.BlockSpec((tm, d), lambda i: (i, 0)),
                  w_spec, w_spec],
        out_specs=pl.BlockSpec((tm, d), lambda i: (i, 0)),
        out_shape=jax.ShapeDtypeStruct((t, d), F32),
        scratch_shapes=[pltpu.VMEM((n_gb * tm, LANES), F32),
                        pltpu.VMEM((tm, d), BF16)],
        compiler_params=_params("arbitrary"),
        name="glu_res",
    )(g_blk, x, w_a_all, w_b_all)


def _ssm_layouts(a_re, a_im, log_dt, b_re, b_im, c_re, c_im):
    g, p = a_re.shape
    sg = b_re.shape[2]
    gl = LANES // sg
    n_gb = g // gl
    dt = jnp.broadcast_to(log_dt[:, None], (g, p))
    prow = jnp.stack([a_re, a_im, dt], axis=0).reshape(3, n_gb, gl * p).transpose(1, 0, 2)

    def b_rows(b):
        return b.reshape(n_gb, gl, p, sg).transpose(0, 3, 1, 2).reshape(n_gb, sg, gl * p)

    def c_cols(c):
        return jnp.tile(c, (1, 1, LANES // p)).reshape(n_gb, gl * sg, LANES)

    return prow, b_rows(b_re), b_rows(b_im), c_cols(c_re), c_cols(c_im)


def _state_to_tiles(st_re, st_im, gl):
    b, g, p = st_re.shape
    n_gb = g // gl
    st = jnp.stack([st_re.reshape(b, n_gb, gl * p), st_im.reshape(b, n_gb, gl * p)], axis=2)
    return st.reshape(b, n_gb * 2 * gl * p)


def _tiles_to_state(x, g, p, gl):
    b = x.shape[0]
    n_gb = g // gl
    x = x.reshape(b, n_gb, 2, gl, p)
    return x[:, :, 0].reshape(b, g, p), x[:, :, 1].reshape(b, g, p)


def kernel(x_prompt, x_sample, cache_attn_k, cache_attn_v, state_ssm_re, state_ssm_im, norm_mix, norm_mlp, norm_final, attn_w_qkv, attn_w_o, attn_rel_bias, ssm_a_re, ssm_a_im, ssm_log_dt, ssm_b_re, ssm_b_im, ssm_c_re, ssm_c_im, ssm_d, ssm_w_glu_a, ssm_w_glu_b, mlp_w_up, mlp_w_down):
    bp, seq, d = x_prompt.shape
    bs, ts, _ = x_sample.shape
    depth = norm_mix.shape[0]
    n_heads, dh = cache_attn_k.shape[3], cache_attn_k.shape[4]
    n_groups, p_state = ssm_a_re.shape[1], ssm_a_re.shape[2]
    sg = d // n_groups
    gl = LANES // sg
    assert ts == CHUNK and seq % KV_TAIL == 0 and (bs * ts) % KV_TAIL == 0
    assert seq % SSM_L == 0 and ts % SSM_L == 0 and d % LANES == 0

    n_prompt_rows = bp * seq
    xs = (x_prompt.reshape(n_prompt_rows, d), x_sample.reshape(bs * ts, d))

    w_qkv = attn_w_qkv.astype(BF16)
    w_o = attn_w_o.astype(BF16)
    w_glu_a = ssm_w_glu_a.astype(BF16)
    w_glu_b = ssm_w_glu_b.astype(BF16)

    k_p, v_p, k_s, v_s = [], [], [], []
    sr_p, si_p, sr_s, si_s = [], [], [], []
    for i in range(depth):
        j = i // 2
        gain = norm_mix[i][None, :]
        if i % 2 == 0:
            qkv, kv_tail = _qkv(xs, gain, w_qkv, j, seq // KV_TAIL, bp, dh ** -0.5 * LOG2_E)
            bias = _bias_table(attn_rel_bias[j])
            o_p = _attn_prompt(qkv, bias, bp, seq, d, dh)
            o_s = _attn_sample(qkv, cache_attn_k, cache_attn_v, j, bias, n_prompt_rows, d, dh)
            x = _proj_res(xs, o_p, o_s, w_o, j)
            n_p, n_s = bp * KV_TAIL, bs * ts
            v0 = n_p + n_s
            k_p.append(kv_tail[:n_p].reshape(bp, KV_TAIL, n_heads, dh))
            k_s.append(kv_tail[n_p:v0].reshape(bs, ts, n_heads, dh))
            v_p.append(kv_tail[v0:v0 + n_p].reshape(bp, KV_TAIL, n_heads, dh))
            v_s.append(kv_tail[v0 + n_p:v0 + n_p + n_s].reshape(bs, ts, n_heads, dh))
        else:
            x = xs[0]
            layouts = _ssm_layouts(ssm_a_re[j], ssm_a_im[j], ssm_log_dt[j], ssm_b_re[j], ssm_b_im[j],
                                   ssm_c_re[j], ssm_c_im[j])
            u_blk = _norm_blocked(x, gain)
            x0 = _state_to_tiles(state_ssm_re[j], state_ssm_im[j], gl)
            g_blk, xf_p, xf_s = _ssm_core(u_blk, x0, *layouts, ssm_d[j][None, :], bp, seq // SSM_L, bs,
                                          ts // SSM_L, gl)
            x = _glu_res(x, g_blk, w_glu_a, w_glu_b, j)
            rp, ip = _tiles_to_state(xf_p, n_groups, p_state, gl)
            rs, is_ = _tiles_to_state(xf_s, n_groups, p_state, gl)
            sr_p.append(rp); si_p.append(ip); sr_s.append(rs); si_s.append(is_)
        x = _mlp(x, norm_mlp[i][None, :], mlp_w_up, mlp_w_down, i)
        xs = (x,)

    y_p, y_s = _final_norm(xs[0], norm_final[None, :], n_prompt_rows)
    return (y_p.reshape(bp, seq, d), y_s.reshape(bs, ts, d),
            jnp.stack(k_p), jnp.stack(v_p), jnp.stack(sr_p), jnp.stack(si_p),
            jnp.stack(k_s), jnp.stack(v_s), jnp.stack(sr_s), jnp.stack(si_s))
```

```python
import functools
import math

import jax
import jax.numpy as jnp
from jax import lax
from jax.experimental import pallas as pl
from jax.experimental.pallas import tpu as pltpu

CHUNK = 64
N_PAST_CHUNKS = 8
RMS_EPS = 1e-5
NEG_INF = -1e30
LOG2_E = math.log2(math.e)

LANES = 128
SUBLANES_BF16 = 16
VMEM_LIMIT_BYTES = 58 * 1024 * 1024

PAIR = 2 * CHUNK
UNIT = 2 * PAIR
KV_TAIL = CHUNK * N_PAST_CHUNKS
WIN = KV_TAIL + UNIT
SSM_L = 16

F32 = jnp.float32
BF16 = jnp.bfloat16


def _params(*sem):
    return pltpu.CompilerParams(dimension_semantics=sem, vmem_limit_bytes=VMEM_LIMIT_BYTES)


def _tile(n, target, mult=SUBLANES_BF16):
    best = None
    for t in range(mult, min(n, target) + 1, mult):
        if n % t == 0:
            best = t
    assert best is not None, (n, target, mult)
    return best


def _rms(x, g):
    y = x * lax.rsqrt(jnp.mean(x * x, axis=-1, keepdims=True) + RMS_EPS)
    return y * g


def _dot(a, b):
    return jnp.dot(a, b, preferred_element_type=F32)


def _dot_nt(a, b):
    return lax.dot_general(a, b, (((1,), (1,)), ((), ())), preferred_element_type=F32)


def _x_specs(xs, tm, d, n_prompt_tiles, grid_rank, row_axis):
    def at(f):
        return lambda *idx: (f(idx[row_axis]), 0)
    if len(xs) == 1:
        return [pl.BlockSpec((tm, d), at(lambda i: i))]
    return [pl.BlockSpec((tm, d), at(lambda i: jnp.minimum(i, n_prompt_tiles - 1))),
            pl.BlockSpec((tm, d), at(lambda i: jnp.maximum(i - n_prompt_tiles, 0)))]


def _read_x(x_refs, i, n_prompt_tiles):
    if len(x_refs) == 1:
        return x_refs[0][...]
    return jnp.where(i < n_prompt_tiles, x_refs[0][...], x_refs[1][...])


def _qkv_kernel(*refs, n_src, n_prompt_tiles, q_scale):
    x_refs, (g_ref, w_ref, o_ref, tail_ref) = refs[:n_src], refs[n_src:]
    x = _read_x(x_refs, pl.program_id(1), n_prompt_tiles)
    hn = _rms(x, g_ref[...]).astype(BF16)
    acc = _dot(hn, w_ref[...])
    scale = jnp.where(pl.program_id(0) == 0, q_scale, 1.0).astype(F32)
    o_ref[...] = (acc * scale).astype(BF16)
    tail_ref[...] = acc


def _qkv(xs, gain, w_all, layer, n_prompt_tiles_per_batch, n_prompt_batches, q_scale):
    d = xs[0].shape[1]
    t = sum(x.shape[0] for x in xs)
    n = w_all.shape[2]
    tm = KV_TAIL
    tn = d
    assert n == 3 * tn
    nt = t // tm
    n_prompt_tiles = n_prompt_tiles_per_batch * n_prompt_batches
    n_slots = n_prompt_batches + (nt - n_prompt_tiles)

    def slot(i):
        return jnp.where(i < n_prompt_tiles, i // n_prompt_tiles_per_batch,
                         i - n_prompt_tiles + n_prompt_batches)

    def tail_block(j, i):
        return jnp.where(j == 0, 2 * n_slots, (j - 1) * n_slots + slot(i)), 0

    return pl.pallas_call(
        functools.partial(_qkv_kernel, n_src=len(xs), n_prompt_tiles=n_prompt_tiles, q_scale=q_scale),
        grid=(n // tn, nt),
        in_specs=_x_specs(xs, tm, d, n_prompt_tiles, 2, 1) + [
            pl.BlockSpec((1, d), lambda j, i: (0, 0)),
            pl.BlockSpec((None, d, tn), lambda j, i: (layer, 0, j))],
        out_specs=[pl.BlockSpec((tm, tn), lambda j, i: (i, j)),
                   pl.BlockSpec((tm, tn), tail_block)],
        out_shape=[jax.ShapeDtypeStruct((t, n), BF16),
                   jax.ShapeDtypeStruct(((2 * n_slots + 1) * tm, tn), F32)],
        compiler_params=_params("arbitrary", "arbitrary"),
        name="qkv",
    )(*xs, gain, w_all)


def _proj_res_kernel(*refs, n_src, n_prompt_tiles):
    x_refs, a_refs, (w_ref, o_ref) = refs[:n_src], refs[n_src:n_src + 2], refs[n_src + 2:]
    i = pl.program_id(0)
    o_ref[...] = _read_x(x_refs, i, n_prompt_tiles) + _dot(_read_x(a_refs, i, n_prompt_tiles), w_ref[...])


def _proj_res(xs, a_prompt, a_sample, w_all, layer):
    d = xs[0].shape[1]
    n_prompt_rows, k = a_prompt.shape
    t = n_prompt_rows + a_sample.shape[0]
    tm = _tile(math.gcd(n_prompt_rows, t - n_prompt_rows), 512)
    npt = n_prompt_rows // tm
    return pl.pallas_call(
        functools.partial(_proj_res_kernel, n_src=len(xs), n_prompt_tiles=npt),
        grid=(t // tm,),
        in_specs=_x_specs(xs, tm, d, npt, 1, 0) + _x_specs((a_prompt, a_sample), tm, k, npt, 1, 0) + [
            pl.BlockSpec((None, k, d), lambda i: (layer, 0, 0), pipeline_mode=pl.Buffered(1))],
        out_specs=pl.BlockSpec((tm, d), lambda i: (i, 0)),
        out_shape=jax.ShapeDtypeStruct((t, d), F32),
        compiler_params=_params("arbitrary"),
        name="proj_res",
    )(*xs, a_prompt, a_sample, w_all)


def _mlp_kernel(x_ref, g_ref, wu_ref, wd_ref, o_ref, hn_ref):
    k = pl.program_id(1)

    @pl.when(k == 0)
    def _():
        x = x_ref[...]
        hn_ref[...] = _rms(x, g_ref[...]).astype(BF16)
        o_ref[...] = x

    h = jnp.maximum(_dot(hn_ref[...], wu_ref[...]), 0.0)
    o_ref[...] += _dot((h * h).astype(BF16), wd_ref[...])


def _mlp_first_kernel(x_ref, g_ref, wu32_ref, wd32_ref, o_ref, wu_ref, wd_ref, hn_ref):
    k = pl.program_id(0)

    @pl.when(k == 0)
    def _():
        x = x_ref[...]
        hn_ref[...] = _rms(x, g_ref[...]).astype(BF16)
        o_ref[...] = x

    wu = wu32_ref[...].astype(BF16)
    wd = wd32_ref[...].astype(BF16)
    wu_ref[...] = wu
    wd_ref[...] = wd
    h = jnp.maximum(_dot(hn_ref[...], wu), 0.0)
    o_ref[...] += _dot((h * h).astype(BF16), wd)


def _mlp(x, gain, w_up_f32, w_down_f32, layer):
    t, d = x.shape
    dff = w_up_f32.shape[2]
    tm = _tile(t, 768)
    tf1 = _tile(dff, 512, LANES)
    x, w_up, w_down = pl.pallas_call(
        _mlp_first_kernel,
        grid=(dff // tf1,),
        in_specs=[pl.BlockSpec((tm, d), lambda k: (0, 0)),
                  pl.BlockSpec((1, d), lambda k: (0, 0)),
                  pl.BlockSpec((None, d, tf1), lambda k: (layer, 0, k)),
                  pl.BlockSpec((None, tf1, d), lambda k: (layer, k, 0))],
        out_specs=[pl.BlockSpec((tm, d), lambda k: (0, 0)),
                   pl.BlockSpec((d, tf1), lambda k: (0, k)),
                   pl.BlockSpec((tf1, d), lambda k: (k, 0))],
        out_shape=[jax.ShapeDtypeStruct((t, d), F32),
                   jax.ShapeDtypeStruct((d, dff), BF16),
                   jax.ShapeDtypeStruct((dff, d), BF16)],
        input_output_aliases={0: 0},
        scratch_shapes=[pltpu.VMEM((tm, d), BF16)],
        compiler_params=_params("arbitrary"),
        name="mlp_first",
    )(x, gain, w_up_f32, w_down_f32)
    if t == tm:
        return x
    tf = _tile(dff, 1024, LANES)
    return pl.pallas_call(
        _mlp_kernel,
        grid=(t // tm - 1, dff // tf),
        in_specs=[pl.BlockSpec((tm, d), lambda i, k: (i + 1, 0)),
                  pl.BlockSpec((1, d), lambda i, k: (0, 0)),
                  pl.BlockSpec((d, tf), lambda i, k: (0, k)),
                  pl.BlockSpec((tf, d), lambda i, k: (k, 0))],
        out_specs=pl.BlockSpec((tm, d), lambda i, k: (i + 1, 0)),
        out_shape=jax.ShapeDtypeStruct((t, d), F32),
        input_output_aliases={0: 0},
        scratch_shapes=[pltpu.VMEM((tm, d), BF16)],
        compiler_params=_params("arbitrary", "arbitrary"),
        name="mlp",
    )(x, gain, w_up, w_down)


def _final_norm_kernel(x_ref, g_ref, yp_ref, ys_ref, *, n_prompt_tiles):
    i = pl.program_id(0)
    y = _rms(x_ref[...], g_ref[...])

    @pl.when(i < n_prompt_tiles)
    def _():
        yp_ref[...] = y

    @pl.when(i >= n_prompt_tiles)
    def _():
        ys_ref[...] = y


def _final_norm(x, gain, n_prompt_rows):
    t, d = x.shape
    n_sample_rows = t - n_prompt_rows
    tm = _tile(math.gcd(n_prompt_rows, n_sample_rows), 512)
    npt = n_prompt_rows // tm
    return pl.pallas_call(
        functools.partial(_final_norm_kernel, n_prompt_tiles=npt),
        grid=(t // tm,),
        in_specs=[pl.BlockSpec((tm, d), lambda i: (i, 0)),
                  pl.BlockSpec((1, d), lambda i: (0, 0))],
        out_specs=[pl.BlockSpec((tm, d), lambda i: (jnp.minimum(i, npt - 1), 0)),
                   pl.BlockSpec((tm, d), lambda i: (jnp.maximum(i - npt, 0), 0))],
        out_shape=[jax.ShapeDtypeStruct((n_prompt_rows, d), F32),
                   jax.ShapeDtypeStruct((n_sample_rows, d), F32)],
        compiler_params=_params("arbitrary"),
        name="final_norm",
    )(x, gain)


def _bias_kernel(t_ref, o_ref):
    n_heads = o_ref.shape[0]
    width = t_ref.shape[1]
    r = lax.broadcasted_iota(jnp.int32, (UNIT, WIN), 0)
    c = lax.broadcasted_iota(jnp.int32, (UNIT, WIN), 1)
    dist = r // PAIR + KV_TAIL // PAIR - c // PAIR
    q_hi = (r % PAIR) >= CHUNK
    k_hi = (c % PAIR) >= CHUNK
    masked = ((dist < 0) | (dist > KV_TAIL // PAIR)
              | ((dist == 0) & jnp.logical_not(q_hi) & k_hi)
              | ((dist == KV_TAIL // PAIR) & q_hi & jnp.logical_not(k_hi)))
    for h in range(n_heads):
        rows = jnp.broadcast_to(t_ref[h:h + 1, :], (UNIT, width))
        rolled = pltpu.roll(rows, width - (UNIT - 1), 1, stride=1, stride_axis=0)
        o_ref[h] = jnp.where(masked, NEG_INF, rolled[:, :WIN] * LOG2_E)


def _bias_table(rel_bias):
    n_heads, n_rel = rel_bias.shape
    max_rel = (n_rel - 1) // 2
    assert max_rel == PAIR
    width = WIN + UNIT
    n_far = WIN - 1 - max_rel + 1
    tvec = jnp.concatenate(
        [jnp.broadcast_to(rel_bias[:, 2 * max_rel:], (n_heads, n_far)),
         jnp.flip(rel_bias[:, :2 * max_rel], axis=1),
         jnp.broadcast_to(rel_bias[:, :1], (n_heads, width - n_far - 2 * max_rel))], axis=1)
    return pl.pallas_call(
        _bias_kernel,
        out_shape=jax.ShapeDtypeStruct((n_heads, UNIT, WIN), F32),
        compiler_params=pltpu.CompilerParams(vmem_limit_bytes=VMEM_LIMIT_BYTES),
        name="attn_bias",
    )(tvec)


def _softmax_pv(scores, values):
    dh = values[0].shape[1]
    m = scores[0].max(axis=1, keepdims=True)
    for s in scores[1:]:
        m = jnp.maximum(m, s.max(axis=1, keepdims=True))
    acc = None
    for s, v in zip(scores, values):
        p = jnp.exp2(s - m).astype(BF16)
        r = _dot(p, jnp.concatenate([v, jnp.ones_like(v)], axis=1))
        acc = r if acc is None else acc + r
    return acc[:, :dh] / acc[:, dh:]


def _attn_prompt_kernel(q_ref, k_ref, v_ref, b_ref, o_ref, *, n_heads, dh, n_units):
    qi = pl.program_id(2)

    def run(items):
        def raw_scores(item):
            q_rows, h, key_rows, _, _ = item
            hs = slice(h * dh, (h + 1) * dh)
            return _dot_nt(q_ref[q_rows, hs], k_ref[key_rows, hs])

        s_next = raw_scores(items[0])
        for n, item in enumerate(items):
            s_raw = s_next
            if n + 1 < len(items):
                s_next = raw_scores(items[n + 1])
            q_rows, h, key_rows, _, b_cols = item
            hs = slice(h * dh, (h + 1) * dh)
            o = _softmax_pv([s_raw + b_ref[h, :, b_cols]], [v_ref[key_rows, hs]])
            o_ref[q_rows, hs] = o.astype(BF16)

    @pl.when(qi == 0)
    def _():
        items = []
        for u in range(n_units):
            nk = (u + 1) * UNIT
            for h in range(n_heads):
                items.append((slice(u * UNIT, (u + 1) * UNIT), h, slice(0, nk), nk, slice(WIN - nk, WIN)))
        run(items)

    @pl.when(qi > 0)
    def _():
        items = []
        for u in range(n_units):
            k0 = pl.multiple_of((qi * n_units + u) * UNIT - KV_TAIL, PAIR)
            for h in range(n_heads):
                items.append((slice(u * UNIT, (u + 1) * UNIT), h, pl.ds(k0, WIN), WIN, slice(0, WIN)))
        run(items)


def _attn_prompt(qkv, bias, n_batches, seq, d, dh):
    n_units = KV_TAIL // UNIT
    tq = n_units * UNIT
    assert seq % tq == 0 and seq >= WIN
    hw = min(d, 4 * dh)
    n_hg = d // hw
    n_heads = hw // dh
    return pl.pallas_call(
        functools.partial(_attn_prompt_kernel, n_heads=n_heads, dh=dh, n_units=n_units),
        grid=(n_batches, n_hg, seq // tq),
        in_specs=[pl.BlockSpec((tq, hw), lambda b, g, i: (b * (seq // tq) + i, g)),
                  pl.BlockSpec((seq, hw), lambda b, g, i: (b, n_hg + g)),
                  pl.BlockSpec((seq, hw), lambda b, g, i: (b, 2 * n_hg + g)),
                  pl.BlockSpec((n_heads, UNIT, WIN), lambda b, g, i: (g, 0, 0))],
        out_specs=pl.BlockSpec((tq, hw), lambda b, g, i: (b * (seq // tq) + i, g)),
        out_shape=jax.ShapeDtypeStruct((n_batches * seq, d), BF16),
        compiler_params=_params("arbitrary", "arbitrary", "arbitrary"),
        name="attn_prompt",
    )(qkv, qkv, qkv, bias)


def _attn_sample_kernel(q_ref, kn_ref, vn_ref, kc_ref, vc_ref, b_ref, o_ref, *, n_heads, dh):
    ts = q_ref.shape[0]
    lc = kc_ref.shape[0] // n_heads

    def cached(ref, h):
        return ref[pl.ds(h, lc, stride=n_heads), :].astype(BF16)

    def raw_scores(h):
        hs = slice(h * dh, (h + 1) * dh)
        q = q_ref[:, hs]
        return _dot_nt(q, cached(kc_ref, h)), _dot_nt(q, kn_ref[:, hs])

    s_next = raw_scores(0)
    for h in range(n_heads):
        s_c, s_n = s_next
        if h + 1 < n_heads:
            s_next = raw_scores(h + 1)
        hs = slice(h * dh, (h + 1) * dh)
        o = _softmax_pv([s_c + b_ref[h, :ts, :lc], s_n + b_ref[h, :ts, lc:lc + ts]],
                        [cached(vc_ref, h), vn_ref[:, hs]])
        o_ref[:, hs] = o.astype(BF16)


def _attn_sample(qkv, k_cache, v_cache, layer, bias, row0, d, dh):
    _, n_b, lc, n_heads, _ = k_cache.shape
    ts = CHUNK
    assert lc == KV_TAIL and row0 % ts == 0 and n_heads * dh == d
    r0 = row0 // ts
    cache_shape = (k_cache.shape[0], n_b, lc * n_heads, dh)
    cache_spec = pl.BlockSpec((None, None, lc * n_heads, dh), lambda b: (layer, b, 0, 0))
    return pl.pallas_call(
        functools.partial(_attn_sample_kernel, n_heads=n_heads, dh=dh),
        grid=(n_b,),
        in_specs=[pl.BlockSpec((ts, d), lambda b: (r0 + b, 0)),
                  pl.BlockSpec((ts, d), lambda b: (r0 + b, 1)),
                  pl.BlockSpec((ts, d), lambda b: (r0 + b, 2)),
                  cache_spec, cache_spec,
                  pl.BlockSpec((n_heads, UNIT, WIN), lambda b: (0, 0, 0))],
        out_specs=pl.BlockSpec((ts, d), lambda b: (b, 0)),
        out_shape=jax.ShapeDtypeStruct((n_b * ts, d), BF16),
        compiler_params=_params("arbitrary"),
        name="attn_sample",
    )(qkv, qkv, qkv, k_cache.reshape(cache_shape), v_cache.reshape(cache_shape), bias)


def _norm_blocked_kernel(x_ref, g_ref, o_ref, slab_scr):
    n_gb = o_ref.shape[0]
    tm = x_ref.shape[0]
    hn = _rms(x_ref[...], g_ref[...])
    for gb in range(n_gb):
        slab_scr[gb * tm:(gb + 1) * tm, :] = hn[:, gb * LANES:(gb + 1) * LANES]
    for gb in range(n_gb):
        for s in range(SSM_L):
            rows = pl.ds(gb * tm + s, tm // SSM_L, stride=SSM_L)
            o_ref[gb, :, s * LANES:(s + 1) * LANES] = slab_scr[rows, :].astype(BF16)


def _norm_blocked(x, gain):
    t, d = x.shape
    nc = t // SSM_L
    tm = _tile(t, 768, SSM_L * SUBLANES_BF16)
    n_gb = d // LANES
    return pl.pallas_call(
        _norm_blocked_kernel,
        grid=(t // tm,),
        in_specs=[pl.BlockSpec((tm, d), lambda i: (i, 0)),
                  pl.BlockSpec((1, d), lambda i: (0, 0))],
        out_specs=pl.BlockSpec((n_gb, tm // SSM_L, SSM_L * LANES), lambda i: (0, i, 0)),
        out_shape=jax.ShapeDtypeStruct((n_gb, nc, SSM_L * LANES), BF16),
        scratch_shapes=[pltpu.VMEM((n_gb * tm, LANES), F32)],
        compiler_params=_params("arbitrary"),
        name="ssm_norm",
    )(x, gain)


def _gelu_tanh(x):
    cdf = 0.5 * (1.0 + jnp.tanh(math.sqrt(2.0 / math.pi) * (x + 0.044715 * (x ** 3))))
    return x * cdf


def _ssm_core_kernel(u_ref, x0_ref, prow_ref, bmr_ref, bmi_ref, cmr_ref, cmi_ref, d_ref,
                     g_ref, xfp_ref, xfs_ref,
                     w_scr, cmt_scr, toep_scr, s_scr, xst_scr,
                     *, n_prompt, blocks_per_prompt, n_sample, blocks_per_sample, row_tile,
                     groups_per_tile):
    n_state = bmr_ref.shape[1]
    ch = cmr_ref.shape[0]

    def over_row_groups(ref):
        return jnp.concatenate([ref[...]] * (ch // ref.shape[0]), axis=0)

    def over_col_groups(ref):
        return jnp.concatenate([ref[...]] * (n_state // ref.shape[1]), axis=1)

    def discretise(a_re, a_im, log_dt):
        dt = jnp.exp(log_dt)
        mag = jnp.exp(dt * a_re)
        l_re = mag * jnp.cos(dt * a_im)
        l_im = mag * jnp.sin(dt * a_im)
        den = a_re * a_re + a_im * a_im
        n_re = l_re - 1.0
        n_im = l_im
        z_re = (n_re * a_re + n_im * a_im) / den
        z_im = (n_im * a_re - n_re * a_im) / den
        return l_re, l_im, z_re, z_im

    def powers(l_re, l_im, n):
        out = [(jnp.ones_like(l_re), jnp.zeros_like(l_im))]
        for _ in range(n):
            pr, pi = out[-1]
            out.append((pr * l_re - pi * l_im, pr * l_im + pi * l_re))
        return out

    lr, li, zr, zi = discretise(prow_ref[0:1, :], prow_ref[1:2, :], prow_ref[2:3, :])
    prow = powers(lr, li, SSM_L)

    p = n_state // groups_per_tile
    sg = ch // groups_per_tile
    mb = (lax.broadcasted_iota(jnp.int32, (ch, n_state), 0) // sg ==
          lax.broadcasted_iota(jnp.int32, (ch, n_state), 1) // p)

    bxr = over_row_groups(bmr_ref)
    bxi = over_row_groups(bmi_ref)
    bb_r = jnp.where(mb, zr * bxr - zi * bxi, 0.0)
    bb_i = jnp.where(mb, zr * bxi + zi * bxr, 0.0)

    for s in range(SSM_L):
        pr, pi = prow[SSM_L - 1 - s]
        rows = slice(s * ch, (s + 1) * ch)
        w_scr[rows, 0:n_state] = (pr * bb_r - pi * bb_i).astype(BF16)
        w_scr[rows, n_state:2 * n_state] = (pr * bb_i + pi * bb_r).astype(BF16)

    cxr = jnp.where(mb, over_col_groups(cmr_ref), 0.0)
    cxi = jnp.where(mb, over_col_groups(cmi_ref), 0.0)
    for tau in range(SSM_L + 1):
        pr, pi = prow[tau]
        rows = slice(tau * ch, (tau + 1) * ch)
        cmt_scr[rows, 0:n_state] = (cxr * pr - cxi * pi).astype(BF16)
        cmt_scr[rows, n_state:2 * n_state] = (-(cxr * pi + cxi * pr)).astype(BF16)

    width = SSM_L * ch
    bbar = jnp.concatenate([bb_r.astype(BF16), bb_i.astype(BF16)], axis=1)
    k_all = _dot_nt(bbar, cmt_scr[0:width, :]).astype(BF16)
    for s in range(SSM_L):
        rows = slice(s * ch, (s + 1) * ch)
        if s % 2 == 1:
            toep_scr[rows, (s - 1) * ch:s * ch] = jnp.zeros((ch, ch), BF16)
        toep_scr[rows, s * ch:width] = k_all[:, 0:width - s * ch]

    n_rows = u_ref.shape[0]
    for r in range(n_rows // row_tile):
        rows = slice(r * row_tile, (r + 1) * row_tile)
        s_scr[rows, :] = _dot(u_ref[rows, :], w_scr[...])

    l16r, l16i = prow[SSM_L]

    def step(xr, xi, sr, si):
        return l16r * xr - l16i * xi + sr, l16r * xi + l16i * xr + si

    def prompt_body(c, carry):
        new = []
        for b in range(n_prompt):
            xr, xi = carry[b]
            row = b * blocks_per_prompt + c
            xst_scr[pl.ds(row, 1), 0:n_state] = xr
            xst_scr[pl.ds(row, 1), n_state:2 * n_state] = xi
            new.append(step(xr, xi, s_scr[pl.ds(row, 1), 0:n_state],
                            s_scr[pl.ds(row, 1), n_state:2 * n_state]))
        return tuple(new)

    zero = jnp.zeros((1, n_state), F32)
    fin = lax.fori_loop(0, blocks_per_prompt, prompt_body, tuple((zero, zero) for _ in range(n_prompt)))
    for b in range(n_prompt):
        xfp_ref[b:b + 1, 0:n_state] = fin[b][0]
        xfp_ref[b:b + 1, n_state:2 * n_state] = fin[b][1]

    base = n_prompt * blocks_per_prompt
    for b in range(n_sample):
        xr = x0_ref[b:b + 1, 0:n_state]
        xi = x0_ref[b:b + 1, n_state:2 * n_state]
        for j in range(blocks_per_sample):
            row = base + b * blocks_per_sample + j
            xst_scr[row:row + 1, 0:n_state] = xr
            xst_scr[row:row + 1, n_state:2 * n_state] = xi
            xr, xi = step(xr, xi, s_scr[row:row + 1, 0:n_state], s_scr[row:row + 1, n_state:2 * n_state])
        xfs_ref[b:b + 1, 0:n_state] = xr
        xfs_ref[b:b + 1, n_state:2 * n_state] = xi

    dvec = d_ref[...]
    for r in range(n_rows // row_tile):
        rows = slice(r * row_tile, (r + 1) * row_tile)
        xst = xst_scr[rows, :].astype(BF16)
        for t in range(0, SSM_L, 2):
            k_ext = (t + 2) * ch
            y = (_dot(u_ref[rows, 0:k_ext], toep_scr[0:k_ext, t * ch:(t + 2) * ch])
                 + _dot_nt(xst, cmt_scr[(t + 1) * ch:(t + 3) * ch, :]))
            for tt in (t, t + 1):
                cols = slice(tt * ch, (tt + 1) * ch)
                yt = y[:, (tt - t) * ch:(tt - t + 1) * ch] + dvec * u_ref[rows, cols].astype(F32)
                g_ref[rows, cols] = _gelu_tanh(yt).astype(BF16)


def _ssm_core(u_blk, x0, prow, bmr, bmi, cmr, cmi, dvec, n_prompt, blocks_per_prompt, n_sample,
              blocks_per_sample, groups_per_tile):
    n_gb, nc, width = u_blk.shape
    ch = LANES
    sg, n_state = bmr.shape[1], bmr.shape[2]
    assert width == SSM_L * ch and nc == n_prompt * blocks_per_prompt + n_sample * blocks_per_sample
    row_tile = _tile(nc, 384)
    kern = functools.partial(_ssm_core_kernel, n_prompt=n_prompt, blocks_per_prompt=blocks_per_prompt,
                             n_sample=n_sample, blocks_per_sample=blocks_per_sample, row_tile=row_tile,
                             groups_per_tile=groups_per_tile)
    return pl.pallas_call(
        kern,
        grid=(n_gb,),
        in_specs=[pl.BlockSpec((None, nc, width), lambda g: (g, 0, 0)),
                  pl.BlockSpec((n_sample, 2 * n_state), lambda g: (0, g)),
                  pl.BlockSpec((None, 3, n_state), lambda g: (g, 0, 0)),
                  pl.BlockSpec((None, sg, n_state), lambda g: (g, 0, 0)),
                  pl.BlockSpec((None, sg, n_state), lambda g: (g, 0, 0)),
                  pl.BlockSpec((None, ch, LANES), lambda g: (g, 0, 0)),
                  pl.BlockSpec((None, ch, LANES), lambda g: (g, 0, 0)),
                  pl.BlockSpec((1, ch), lambda g: (0, g))],
        out_specs=[pl.BlockSpec((None, nc, width), lambda g: (g, 0, 0)),
                   pl.BlockSpec((n_prompt, 2 * n_state), lambda g: (0, g)),
                   pl.BlockSpec((n_sample, 2 * n_state), lambda g: (0, g))],
        out_shape=[jax.ShapeDtypeStruct((n_gb, nc, width), BF16),
                   jax.ShapeDtypeStruct((n_prompt, n_gb * 2 * n_state), F32),
                   jax.ShapeDtypeStruct((n_sample, n_gb * 2 * n_state), F32)],
        scratch_shapes=[pltpu.VMEM((width, 2 * n_state), BF16),
                        pltpu.VMEM((width + ch, 2 * n_state), BF16),
                        pltpu.VMEM((width, width), BF16),
                        pltpu.VMEM((nc, 2 * n_state), F32),
                        pltpu.VMEM((nc, 2 * n_state), F32)],
        compiler_params=_params("arbitrary"),
        name="ssm_core",
    )(u_blk, x0, prow, bmr, bmi, cmr, cmi, dvec)


def _glu_res_kernel(g_ref, x_ref, wa_ref, wb_ref, o_ref, slab_scr, lhs_scr):
    n_gb = g_ref.shape[0]
    tm = x_ref.shape[0]

    for gb in range(n_gb):
        for s in range(SSM_L):
            rows = pl.ds(gb * tm + s, tm // SSM_L, stride=SSM_L)
            slab_scr[rows, :] = g_ref[gb, :, s * LANES:(s + 1) * LANES].astype(F32)
    for gb in range(n_gb):
        lhs_scr[:, gb * LANES:(gb + 1) * LANES] = slab_scr[gb * tm:(gb + 1) * tm, :].astype(BF16)

    g = lhs_scr[...]
    a = _dot(g, wa_ref[...])
    b = _dot(g, wb_ref[...])
    o_ref[...] = x_ref[...] + a * jax.nn.sigmoid(b)


def _glu_res(x, g_blk, w_a_all, w_b_all, layer):
    t, d = x.shape
    n_gb, nc, _ = g_blk.shape
    tm = _tile(t, 512, SSM_L * SUBLANES_BF16)
    w_spec = pl.BlockSpec((None, d, d), lambda i: (layer, 0, 0), pipeline_mode=pl.Buffered(1))
    return pl.pallas_call(
        _glu_res_kernel,
        grid=(t // tm,),
        in_specs=[pl.BlockSpec((n_gb, tm // SSM_L, SSM_L * LANES), lambda i: (0, i, 0)),
                  pl.BlockSpec((tm, d), lambda i: (i, 0)),
                  w_spec, w_spec],
        out_specs=pl.BlockSpec((tm, d), lambda i: (i, 0)),
        out_shape=jax.ShapeDtypeStruct((t, d), F32),
        scratch_shapes=[pltpu.VMEM((n_gb * tm, LANES), F32),
                        pltpu.VMEM((tm, d), BF16)],
        compiler_params=_params("arbitrary"),
        name="glu_res",
    )(g_blk, x, w_a_all, w_b_all)


def _ssm_layouts(a_re, a_im, log_dt, b_re, b_im, c_re, c_im):
    g, p = a_re.shape
    sg = b_re.shape[2]
    gl = LANES // sg
    n_gb = g // gl
    dt = jnp.broadcast_to(log_dt[:, None], (g, p))
    prow = jnp.stack([a_re, a_im, dt], axis=0).reshape(3, n_gb, gl * p).transpose(1, 0, 2)

    def b_rows(b):
        return b.reshape(n_gb, gl, p, sg).transpose(0, 3, 1, 2).reshape(n_gb, sg, gl * p)

    def c_cols(c):
        return jnp.tile(c, (1, 1, LANES // p)).reshape(n_gb, gl * sg, LANES)

    return prow, b_rows(b_re), b_rows(b_im), c_cols(c_re), c_cols(c_im)


def _state_to_tiles(st_re, st_im, gl):
    b, g, p = st_re.shape
    n_gb = g // gl
    st = jnp.stack([st_re.reshape(b, n_gb, gl * p), st_im.reshape(b, n_gb, gl * p)], axis=2)
    return st.reshape(b, n_gb * 2 * gl * p)


def _tiles_to_state(x, g, p, gl):
    b = x.shape[0]
    n_gb = g // gl
    x = x.reshape(b, n_gb, 2, gl, p)
    return x[:, :, 0].reshape(b, g, p), x[:, :, 1].reshape(b, g, p)


def kernel(x_prompt, x_sample, cache_attn_k, cache_attn_v, state_ssm_re, state_ssm_im, norm_mix, norm_mlp, norm_final, attn_w_qkv, attn_w_o, attn_rel_bias, ssm_a_re, ssm_a_im, ssm_log_dt, ssm_b_re, ssm_b_im, ssm_c_re, ssm_c_im, ssm_d, ssm_w_glu_a, ssm_w_glu_b, mlp_w_up, mlp_w_down):
    bp, seq, d = x_prompt.shape
    bs, ts, _ = x_sample.shape
    depth = norm_mix.shape[0]
    n_heads, dh = cache_attn_k.shape[3], cache_attn_k.shape[4]
    n_groups, p_state = ssm_a_re.shape[1], ssm_a_re.shape[2]
    sg = d // n_groups
    gl = LANES // sg
    assert ts == CHUNK and seq % KV_TAIL == 0 and (bs * ts) % KV_TAIL == 0
    assert seq % SSM_L == 0 and ts % SSM_L == 0 and d % LANES == 0

    n_prompt_rows = bp * seq
    xs = (x_prompt.reshape(n_prompt_rows, d), x_sample.reshape(bs * ts, d))

    w_qkv = attn_w_qkv.astype(BF16)
    w_o = attn_w_o.astype(BF16)
    w_glu_a = ssm_w_glu_a.astype(BF16)
    w_glu_b = ssm_w_glu_b.astype(BF16)

    k_p, v_p, k_s, v_s = [], [], [], []
    sr_p, si_p, sr_s, si_s = [], [], [], []
    for i in range(depth):
        j = i // 2
        gain = norm_mix[i][None, :]
        if i % 2 == 0:
            qkv, kv_tail = _qkv(xs, gain, w_qkv, j, seq // KV_TAIL, bp, dh ** -0.5 * LOG2_E)
            bias = _bias_table(attn_rel_bias[j])
            o_p = _attn_prompt(qkv, bias, bp, seq, d, dh)
            o_s = _attn_sample(qkv, cache_attn_k, cache_attn_v, j, bias, n_prompt_rows, d, dh)
            x = _proj_res(xs, o_p, o_s, w_o, j)
            n_p, n_s = bp * KV_TAIL, bs * ts
            v0 = n_p + n_s
            k_p.append(kv_tail[:n_p].reshape(bp, KV_TAIL, n_heads, dh))
            k_s.append(kv_tail[n_p:v0].reshape(bs, ts, n_heads, dh))
            v_p.append(kv_tail[v0:v0 + n_p].reshape(bp, KV_TAIL, n_heads, dh))
            v_s.append(kv_tail[v0 + n_p:v0 + n_p + n_s].reshape(bs, ts, n_heads, dh))
        else:
            x = xs[0]
            layouts = _ssm_layouts(ssm_a_re[j], ssm_a_im[j], ssm_log_dt[j], ssm_b_re[j], ssm_b_im[j],
                                   ssm_c_re[j], ssm_c_im[j])
            u_blk = _norm_blocked(x, gain)
            x0 = _state_to_tiles(state_ssm_re[j], state_ssm_im[j], gl)
            g_blk, xf_p, xf_s = _ssm_core(u_blk, x0, *layouts, ssm_d[j][None, :], bp, seq // SSM_L, bs,
                                          ts // SSM_L, gl)
            x = _glu_res(x, g_blk, w_glu_a, w_glu_b, j)
            rp, ip = _tiles_to_state(xf_p, n_groups, p_state, gl)
            rs, is_ = _tiles_to_state(xf_s, n_groups, p_state, gl)
            sr_p.append(rp); si_p.append(ip); sr_s.append(rs); si_s.append(is_)
        x = _mlp(x, norm_mlp[i][None, :], mlp_w_up, mlp_w_down, i)
        xs = (x,)

    y_p, y_s = _final_norm(xs[0], norm_final[None, :], n_prompt_rows)
    return (y_p.reshape(bp, seq, d), y_s.reshape(bs, ts, d),
            jnp.stack(k_p), jnp.stack(v_p), jnp.stack(sr_p), jnp.stack(si_p),
            jnp.stack(k_s), jnp.stack(v_s), jnp.stack(sr_s), jnp.stack(si_s))
```

```python
import functools
import math

import jax
import jax.numpy as jnp
from jax import lax
from jax.experimental import pallas as pl
from jax.experimental.pallas import tpu as pltpu

CHUNK = 64
N_PAST_CHUNKS = 8
RMS_EPS = 1e-5
NEG_INF = -1e30
LOG2_E = math.log2(math.e)

LANES = 128
SUBLANES_BF16 = 16
VMEM_LIMIT_BYTES = 58 * 1024 * 1024

PAIR = 2 * CHUNK
UNIT = 2 * PAIR
KV_TAIL = CHUNK * N_PAST_CHUNKS
WIN = KV_TAIL + UNIT
SSM_L = 16

F32 = jnp.float32
BF16 = jnp.bfloat16


def _params(*sem):
    return pltpu.CompilerParams(dimension_semantics=sem, vmem_limit_bytes=VMEM_LIMIT_BYTES)


def _tile(n, target, mult=SUBLANES_BF16):
    best = None
    for t in range(mult, min(n, target) + 1, mult):
        if n % t == 0:
            best = t
    assert best is not None, (n, target, mult)
    return best


def _rms(x, g):
    y = x * lax.rsqrt(jnp.mean(x * x, axis=-1, keepdims=True) + RMS_EPS)
    return y * g


def _dot(a, b):
    return jnp.dot(a, b, preferred_element_type=F32)


def _dot_nt(a, b):
    return lax.dot_general(a, b, (((1,), (1,)), ((), ())), preferred_element_type=F32)


def _x_specs(xs, tm, d, n_prompt_tiles, grid_rank, row_axis):
    def at(f):
        return lambda *idx: (f(idx[row_axis]), 0)
    if len(xs) == 1:
        return [pl.BlockSpec((tm, d), at(lambda i: i))]
    return [pl.BlockSpec((tm, d), at(lambda i: jnp.minimum(i, n_prompt_tiles - 1))),
            pl.BlockSpec((tm, d), at(lambda i: jnp.maximum(i - n_prompt_tiles, 0)))]


def _read_x(x_refs, i, n_prompt_tiles):
    if len(x_refs) == 1:
        return x_refs[0][...]
    return jnp.where(i < n_prompt_tiles, x_refs[0][...], x_refs[1][...])


def _qkv_kernel(*refs, n_src, n_prompt_tiles, q_scale):
    x_refs, (g_ref, w_ref, o_ref, tail_ref) = refs[:n_src], refs[n_src:]
    x = _read_x(x_refs, pl.program_id(1), n_prompt_tiles)
    hn = _rms(x, g_ref[...]).astype(BF16)
    acc = _dot(hn, w_ref[...])
    scale = jnp.where(pl.program_id(0) == 0, q_scale, 1.0).astype(F32)
    o_ref[...] = (acc * scale).astype(BF16)
    tail_ref[...] = acc


def _qkv(xs, gain, w_all, layer, n_prompt_tiles_per_batch, n_prompt_batches, q_scale):
    d = xs[0].shape[1]
    t = sum(x.shape[0] for x in xs)
    n = w_all.shape[2]
    tm = KV_TAIL
    tn = d
    assert n == 3 * tn
    nt = t // tm
    n_prompt_tiles = n_prompt_tiles_per_batch * n_prompt_batches
    n_slots = n_prompt_batches + (nt - n_prompt_tiles)

    def slot(i):
        return jnp.where(i < n_prompt_tiles, i // n_prompt_tiles_per_batch,
                         i - n_prompt_tiles + n_prompt_batches)

    def tail_block(j, i):
        return jnp.where(j == 0, 2 * n_slots, (j - 1) * n_slots + slot(i)), 0

    return pl.pallas_call(
        functools.partial(_qkv_kernel, n_src=len(xs), n_prompt_tiles=n_prompt_tiles, q_scale=q_scale),
        grid=(n // tn, nt),
        in_specs=_x_specs(xs, tm, d, n_prompt_tiles, 2, 1) + [
            pl.BlockSpec((1, d), lambda j, i: (0, 0)),
            pl.BlockSpec((None, d, tn), lambda j, i: (layer, 0, j))],
        out_specs=[pl.BlockSpec((tm, tn), lambda j, i: (i, j)),
                   pl.BlockSpec((tm, tn), tail_block)],
        out_shape=[jax.ShapeDtypeStruct((t, n), BF16),
                   jax.ShapeDtypeStruct(((2 * n_slots + 1) * tm, tn), F32)],
        compiler_params=_params("arbitrary", "arbitrary"),
        name="qkv",
    )(*xs, gain, w_all)


def _proj_res_kernel(*refs, n_src, n_prompt_tiles):
    x_refs, a_refs, (w_ref, o_ref) = refs[:n_src], refs[n_src:n_src + 2], refs[n_src + 2:]
    i = pl.program_id(0)
    o_ref[...] = _read_x(x_refs, i, n_prompt_tiles) + _dot(_read_x(a_refs, i, n_prompt_tiles), w_ref[...])


def _proj_res(xs, a_prompt, a_sample, w_all, layer):
    d = xs[0].shape[1]
    n_prompt_rows, k = a_prompt.shape
    t = n_prompt_rows + a_sample.shape[0]
    tm = _tile(math.gcd(n_prompt_rows, t - n_prompt_rows), 512)
    npt = n_prompt_rows // tm
    return pl.pallas_call(
        functools.partial(_proj_res_kernel, n_src=len(xs), n_prompt_tiles=npt),
        grid=(t // tm,),
        in_specs=_x_specs(xs, tm, d, npt, 1, 0) + _x_specs((a_prompt, a_sample), tm, k, npt, 1, 0) + [
            pl.BlockSpec((None, k, d), lambda i: (layer, 0, 0), pipeline_mode=pl.Buffered(1))],
        out_specs=pl.BlockSpec((tm, d), lambda i: (i, 0)),
        out_shape=jax.ShapeDtypeStruct((t, d), F32),
        compiler_params=_params("arbitrary"),
        name="proj_res",
    )(*xs, a_prompt, a_sample, w_all)


def _mlp_kernel(x_ref, g_ref, wu_ref, wd_ref, o_ref, hn_ref):
    k = pl.program_id(1)

    @pl.when(k == 0)
    def _():
        x = x_ref[...]
        hn_ref[...] = _rms(x, g_ref[...]).astype(BF16)
        o_ref[...] = x

    h = jnp.maximum(_dot(hn_ref[...], wu_ref[...]), 0.0)
    o_ref[...] += _dot((h * h).astype(BF16), wd_ref[...])


def _mlp_first_kernel(x_ref, g_ref, wu32_ref, wd32_ref, o_ref, wu_ref, wd_ref, hn_ref):
    k = pl.program_id(0)

    @pl.when(k == 0)
    def _():
        x = x_ref[...]
        hn_ref[...] = _rms(x, g_ref[...]).astype(BF16)
        o_ref[...] = x

    wu = wu32_ref[...].astype(BF16)
    wd = wd32_ref[...].astype(BF16)
    wu_ref[...] = wu
    wd_ref[...] = wd
    h = jnp.maximum(_dot(hn_ref[...], wu), 0.0)
    o_ref[...] += _dot((h * h).astype(BF16), wd)


def _mlp(x, gain, w_up_f32, w_down_f32, layer):
    t, d = x.shape
    dff = w_up_f32.shape[2]
    tm = _tile(t, 768)
    tf1 = _tile(dff, 512, LANES)
    x, w_up, w_down = pl.pallas_call(
        _mlp_first_kernel,
        grid=(dff // tf1,),
        in_specs=[pl.BlockSpec((tm, d), lambda k: (0, 0)),
                  pl.BlockSpec((1, d), lambda k: (0, 0)),
                  pl.BlockSpec((None, d, tf1), lambda k: (layer, 0, k)),
                  pl.BlockSpec((None, tf1, d), lambda k: (layer, k, 0))],
        out_specs=[pl.BlockSpec((tm, d), lambda k: (0, 0)),
                   pl.BlockSpec((d, tf1), lambda k: (0, k)),
                   pl.BlockSpec((tf1, d), lambda k: (k, 0))],
        out_shape=[jax.ShapeDtypeStruct((t, d), F32),
                   jax.ShapeDtypeStruct((d, dff), BF16),
                   jax.ShapeDtypeStruct((dff, d), BF16)],
        input_output_aliases={0: 0},
        scratch_shapes=[pltpu.VMEM((tm, d), BF16)],
        compiler_params=_params("arbitrary"),
        name="mlp_first",
    )(x, gain, w_up_f32, w_down_f32)
    if t == tm:
        return x
    tf = _tile(dff, 1024, LANES)
    return pl.pallas_call(
        _mlp_kernel,
        grid=(t // tm - 1, dff // tf),
        in_specs=[pl.BlockSpec((tm, d), lambda i, k: (i + 1, 0)),
                  pl.BlockSpec((1, d), lambda i, k: (0, 0)),
                  pl.BlockSpec((d, tf), lambda i, k: (0, k)),
                  pl.BlockSpec((tf, d), lambda i, k: (k, 0))],
        out_specs=pl.BlockSpec((tm, d), lambda i, k: (i + 1, 0)),
        out_shape=jax.ShapeDtypeStruct((t, d), F32),
        input_output_aliases={0: 0},
        scratch_shapes=[pltpu.VMEM((tm, d), BF16)],
        compiler_params=_params("arbitrary", "arbitrary"),
        name="mlp",
    )(x, gain, w_up, w_down)


def _final_norm_kernel(x_ref, g_ref, yp_ref, ys_ref, *, n_prompt_tiles):
    i = pl.program_id(0)
    y = _rms(x_ref[...], g_ref[...])

    @pl.when(i < n_prompt_tiles)
    def _():
        yp_ref[...] = y

    @pl.when(i >= n_prompt_tiles)
    def _():
        ys_ref[...] = y


def _final_norm(x, gain, n_prompt_rows):
    t, d = x.shape
    n_sample_rows = t - n_prompt_rows
    tm = _tile(math.gcd(n_prompt_rows, n_sample_rows), 512)
    npt = n_prompt_rows // tm
    return pl.pallas_call(
        functools.partial(_final_norm_kernel, n_prompt_tiles=npt),
        grid=(t // tm,),
        in_specs=[pl.BlockSpec((tm, d), lambda i: (i, 0)),
                  pl.BlockSpec((1, d), lambda i: (0, 0))],
        out_specs=[pl.BlockSpec((tm, d), lambda i: (jnp.minimum(i, npt - 1), 0)),
                   pl.BlockSpec((tm, d), lambda i: (jnp.maximum(i - npt, 0), 0))],
        out_shape=[jax.ShapeDtypeStruct((n_prompt_rows, d), F32),
                   jax.ShapeDtypeStruct((n_sample_rows, d), F32)],
        compiler_params=_params("arbitrary"),
        name="final_norm",
    )(x, gain)


def _bias_kernel(t_ref, o_ref):
    n_heads = o_ref.shape[0]
    width = t_ref.shape[1]
    r = lax.broadcasted_iota(jnp.int32, (UNIT, WIN), 0)
    c = lax.broadcasted_iota(jnp.int32, (UNIT, WIN), 1)
    dist = r // PAIR + KV_TAIL // PAIR - c // PAIR
    q_hi = (r % PAIR) >= CHUNK
    k_hi = (c % PAIR) >= CHUNK
    masked = ((dist < 0) | (dist > KV_TAIL // PAIR)
              | ((dist == 0) & jnp.logical_not(q_hi) & k_hi)
              | ((dist == KV_TAIL // PAIR) & q_hi & jnp.logical_not(k_hi)))
    for h in range(n_heads):
        rows = jnp.broadcast_to(t_ref[h:h + 1, :], (UNIT, width))
        rolled = pltpu.roll(rows, width - (UNIT - 1), 1, stride=1, stride_axis=0)
        o_ref[h] = jnp.where(masked, NEG_INF, rolled[:, :WIN] * LOG2_E)


def _bias_table(rel_bias):
    n_heads, n_rel = rel_bias.shape
    max_rel = (n_rel - 1) // 2
    assert max_rel == PAIR
    width = WIN + UNIT
    n_far = WIN - 1 - max_rel + 1
    tvec = jnp.concatenate(
        [jnp.broadcast_to(rel_bias[:, 2 * max_rel:], (n_heads, n_far)),
         jnp.flip(rel_bias[:, :2 * max_rel], axis=1),
         jnp.broadcast_to(rel_bias[:, :1], (n_heads, width - n_far - 2 * max_rel))], axis=1)
    return pl.pallas_call(
        _bias_kernel,
        out_shape=jax.ShapeDtypeStruct((n_heads, UNIT, WIN), F32),
        compiler_params=pltpu.CompilerParams(vmem_limit_bytes=VMEM_LIMIT_BYTES),
        name="attn_bias",
    )(tvec)


def _softmax_pv(scores, values):
    dh = values[0].shape[1]
    m = scores[0].max(axis=1, keepdims=True)
    for s in scores[1:]:
        m = jnp.maximum(m, s.max(axis=1, keepdims=True))
    acc = None
    for s, v in zip(scores, values):
        p = jnp.exp2(s - m).astype(BF16)
        r = _dot(p, jnp.concatenate([v, jnp.ones_like(v)], axis=1))
        acc = r if acc is None else acc + r
    return acc[:, :dh] / acc[:, dh:]


def _attn_prompt_kernel(q_ref, k_ref, v_ref, b_ref, o_ref, *, n_heads, dh, n_units):
    qi = pl.program_id(2)

    def run(items):
        def raw_scores(item):
            q_rows, h, key_rows, _, _ = item
            hs = slice(h * dh, (h + 1) * dh)
            return _dot_nt(q_ref[q_rows, hs], k_ref[key_rows, hs])

        s_next = raw_scores(items[0])
        for n, item in enumerate(items):
            s_raw = s_next
            if n + 1 < len(items):
                s_next = raw_scores(items[n + 1])
            q_rows, h, key_rows, _, b_cols = item
            hs = slice(h * dh, (h + 1) * dh)
            o = _softmax_pv([s_raw + b_ref[h, :, b_cols]], [v_ref[key_rows, hs]])
            o_ref[q_rows, hs] = o.astype(BF16)

    @pl.when(qi == 0)
    def _():
        items = []
        for u in range(n_units):
            k0 = max(u * UNIT - KV_TAIL, 0)
            nk = (u + 1) * UNIT - k0
            for h in range(n_heads):
                items.append((slice(u * UNIT, (u + 1) * UNIT), h, slice(k0, k0 + nk), nk, slice(WIN - nk, WIN)))
        run(items)

    @pl.when(qi > 0)
    def _():
        items = []
        for u in range(n_units):
            k0 = pl.multiple_of((qi * n_units + u) * UNIT - KV_TAIL, PAIR)
            for h in range(n_heads):
                items.append((slice(u * UNIT, (u + 1) * UNIT), h, pl.ds(k0, WIN), WIN, slice(0, WIN)))
        run(items)


def _attn_prompt(qkv, bias, n_batches, seq, d, dh):
    n_units = 2 * KV_TAIL // UNIT
    tq = n_units * UNIT
    assert seq % tq == 0 and tq >= KV_TAIL
    hw = min(d, 4 * dh)
    n_hg = d // hw
    n_heads = hw // dh
    return pl.pallas_call(
        functools.partial(_attn_prompt_kernel, n_heads=n_heads, dh=dh, n_units=n_units),
        grid=(n_batches, n_hg, seq // tq),
        in_specs=[pl.BlockSpec((tq, hw), lambda b, g, i: (b * (seq // tq) + i, g)),
                  pl.BlockSpec((seq, hw), lambda b, g, i: (b, n_hg + g)),
                  pl.BlockSpec((seq, hw), lambda b, g, i: (b, 2 * n_hg + g)),
                  pl.BlockSpec((n_heads, UNIT, WIN), lambda b, g, i: (g, 0, 0))],
        out_specs=pl.BlockSpec((tq, hw), lambda b, g, i: (b * (seq // tq) + i, g)),
        out_shape=jax.ShapeDtypeStruct((n_batches * seq, d), BF16),
        compiler_params=_params("arbitrary", "arbitrary", "arbitrary"),
        name="attn_prompt",
    )(qkv, qkv, qkv, bias)


def _attn_sample_kernel(q_ref, kn_ref, vn_ref, kc_ref, vc_ref, b_ref, o_ref, *, n_heads, dh):
    ts = q_ref.shape[0]
    lc = kc_ref.shape[0] // n_heads

    def cached(ref, h):
        return ref[pl.ds(h, lc, stride=n_heads), :].astype(BF16)

    def raw_scores(h):
        hs = slice(h * dh, (h + 1) * dh)
        q = q_ref[:, hs]
        return _dot_nt(q, cached(kc_ref, h)), _dot_nt(q, kn_ref[:, hs])

    s_next = raw_scores(0)
    for h in range(n_heads):
        s_c, s_n = s_next
        if h + 1 < n_heads:
            s_next = raw_scores(h + 1)
        hs = slice(h * dh, (h + 1) * dh)
        o = _softmax_pv([s_c + b_ref[h, :ts, :lc], s_n + b_ref[h, :ts, lc:lc + ts]],
                        [cached(vc_ref, h), vn_ref[:, hs]])
        o_ref[:, hs] = o.astype(BF16)


def _attn_sample(qkv, k_cache, v_cache, layer, bias, row0, d, dh):
    _, n_b, lc, n_heads, _ = k_cache.shape
    ts = CHUNK
    assert lc == KV_TAIL and row0 % ts == 0 and n_heads * dh == d
    r0 = row0 // ts
    cache_shape = (k_cache.shape[0], n_b, lc * n_heads, dh)
    cache_spec = pl.BlockSpec((None, None, lc * n_heads, dh), lambda b: (layer, b, 0, 0))
    return pl.pallas_call(
        functools.partial(_attn_sample_kernel, n_heads=n_heads, dh=dh),
        grid=(n_b,),
        in_specs=[pl.BlockSpec((ts, d), lambda b: (r0 + b, 0)),
                  pl.BlockSpec((ts, d), lambda b: (r0 + b, 1)),
                  pl.BlockSpec((ts, d), lambda b: (r0 + b, 2)),
                  cache_spec, cache_spec,
                  pl.BlockSpec((n_heads, UNIT, WIN), lambda b: (0, 0, 0))],
        out_specs=pl.BlockSpec((ts, d), lambda b: (b, 0)),
        out_shape=jax.ShapeDtypeStruct((n_b * ts, d), BF16),
        compiler_params=_params("arbitrary"),
        name="attn_sample",
    )(qkv, qkv, qkv, k_cache.reshape(cache_shape), v_cache.reshape(cache_shape), bias)


def _norm_blocked_kernel(x_ref, g_ref, o_ref, slab_scr):
    n_gb = o_ref.shape[0]
    tm = x_ref.shape[0]
    hn = _rms(x_ref[...], g_ref[...])
    for gb in range(n_gb):
        slab_scr[gb * tm:(gb + 1) * tm, :] = hn[:, gb * LANES:(gb + 1) * LANES]
    for gb in range(n_gb):
        for s in range(SSM_L):
            rows = pl.ds(gb * tm + s, tm // SSM_L, stride=SSM_L)
            o_ref[gb, :, s * LANES:(s + 1) * LANES] = slab_scr[rows, :].astype(BF16)


def _norm_blocked(x, gain):
    t, d = x.shape
    nc = t // SSM_L
    tm = _tile(t, 768, SSM_L * SUBLANES_BF16)
    n_gb = d // LANES
    return pl.pallas_call(
        _norm_blocked_kernel,
        grid=(t // tm,),
        in_specs=[pl.BlockSpec((tm, d), lambda i: (i, 0)),
                  pl.BlockSpec((1, d), lambda i: (0, 0))],
        out_specs=pl.BlockSpec((n_gb, tm // SSM_L, SSM_L * LANES), lambda i: (0, i, 0)),
        out_shape=jax.ShapeDtypeStruct((n_gb, nc, SSM_L * LANES), BF16),
        scratch_shapes=[pltpu.VMEM((n_gb * tm, LANES), F32)],
        compiler_params=_params("arbitrary"),
        name="ssm_norm",
    )(x, gain)


def _gelu_tanh(x):
    cdf = 0.5 * (1.0 + jnp.tanh(math.sqrt(2.0 / math.pi) * (x + 0.044715 * (x ** 3))))
    return x * cdf


def _ssm_core_kernel(u_ref, x0_ref, prow_ref, bmr_ref, bmi_ref, cmr_ref, cmi_ref, d_ref,
                     g_ref, xfp_ref, xfs_ref,
                     w_scr, cmt_scr, toep_scr, s_scr, xst_scr,
                     *, n_prompt, blocks_per_prompt, n_sample, blocks_per_sample, row_tile,
                     groups_per_tile):
    n_state = bmr_ref.shape[1]
    ch = cmr_ref.shape[0]

    def over_row_groups(ref):
        return jnp.concatenate([ref[...]] * (ch // ref.shape[0]), axis=0)

    def over_col_groups(ref):
        return jnp.concatenate([ref[...]] * (n_state // ref.shape[1]), axis=1)

    def discretise(a_re, a_im, log_dt):
        dt = jnp.exp(log_dt)
        mag = jnp.exp(dt * a_re)
        l_re = mag * jnp.cos(dt * a_im)
        l_im = mag * jnp.sin(dt * a_im)
        den = a_re * a_re + a_im * a_im
        n_re = l_re - 1.0
        n_im = l_im
        z_re = (n_re * a_re + n_im * a_im) / den
        z_im = (n_im * a_re - n_re * a_im) / den
        return l_re, l_im, z_re, z_im

    def powers(l_re, l_im, n):
        out = [(jnp.ones_like(l_re), jnp.zeros_like(l_im))]
        for _ in range(n):
            pr, pi = out[-1]
            out.append((pr * l_re - pi * l_im, pr * l_im + pi * l_re))
        return out

    lr, li, zr, zi = discretise(prow_ref[0:1, :], prow_ref[1:2, :], prow_ref[2:3, :])
    prow = powers(lr, li, SSM_L)

    p = n_state // groups_per_tile
    sg = ch // groups_per_tile
    mb = (lax.broadcasted_iota(jnp.int32, (ch, n_state), 0) // sg ==
          lax.broadcasted_iota(jnp.int32, (ch, n_state), 1) // p)

    bxr = over_row_groups(bmr_ref)
    bxi = over_row_groups(bmi_ref)
    bb_r = jnp.where(mb, zr * bxr - zi * bxi, 0.0)
    bb_i = jnp.where(mb, zr * bxi + zi * bxr, 0.0)

    for s in range(SSM_L):
        pr, pi = prow[SSM_L - 1 - s]
        rows = slice(s * ch, (s + 1) * ch)
        w_scr[rows, 0:n_state] = (pr * bb_r - pi * bb_i).astype(BF16)
        w_scr[rows, n_state:2 * n_state] = (pr * bb_i + pi * bb_r).astype(BF16)

    cxr = jnp.where(mb, over_col_groups(cmr_ref), 0.0)
    cxi = jnp.where(mb, over_col_groups(cmi_ref), 0.0)
    for tau in range(SSM_L + 1):
        pr, pi = prow[tau]
        rows = slice(tau * ch, (tau + 1) * ch)
        cmt_scr[rows, 0:n_state] = (cxr * pr - cxi * pi).astype(BF16)
        cmt_scr[rows, n_state:2 * n_state] = (-(cxr * pi + cxi * pr)).astype(BF16)

    width = SSM_L * ch
    bbar = jnp.concatenate([bb_r.astype(BF16), bb_i.astype(BF16)], axis=1)
    k_all = _dot_nt(bbar, cmt_scr[0:width, :]).astype(BF16)
    for s in range(SSM_L):
        rows = slice(s * ch, (s + 1) * ch)
        if s % 2 == 1:
            toep_scr[rows, (s - 1) * ch:s * ch] = jnp.zeros((ch, ch), BF16)
        toep_scr[rows, s * ch:width] = k_all[:, 0:width - s * ch]

    n_rows = u_ref.shape[0]
    for r in range(n_rows // row_tile):
        rows = slice(r * row_tile, (r + 1) * row_tile)
        s_scr[rows, :] = _dot(u_ref[rows, :], w_scr[...])

    l16r, l16i = prow[SSM_L]

    def step(xr, xi, sr, si):
        return l16r * xr - l16i * xi + sr, l16r * xi + l16i * xr + si

    def prompt_body(c, carry):
        new = []
        for b in range(n_prompt):
            xr, xi = carry[b]
            row = b * blocks_per_prompt + c
            xst_scr[pl.ds(row, 1), 0:n_state] = xr
            xst_scr[pl.ds(row, 1), n_state:2 * n_state] = xi
            new.append(step(xr, xi, s_scr[pl.ds(row, 1), 0:n_state],
                            s_scr[pl.ds(row, 1), n_state:2 * n_state]))
        return tuple(new)

    zero = jnp.zeros((1, n_state), F32)
    fin = lax.fori_loop(0, blocks_per_prompt, prompt_body, tuple((zero, zero) for _ in range(n_prompt)))
    for b in range(n_prompt):
        xfp_ref[b:b + 1, 0:n_state] = fin[b][0]
        xfp_ref[b:b + 1, n_state:2 * n_state] = fin[b][1]

    base = n_prompt * blocks_per_prompt
    for b in range(n_sample):
        xr = x0_ref[b:b + 1, 0:n_state]
        xi = x0_ref[b:b + 1, n_state:2 * n_state]
        for j in range(blocks_per_sample):
            row = base + b * blocks_per_sample + j
            xst_scr[row:row + 1, 0:n_state] = xr
            xst_scr[row:row + 1, n_state:2 * n_state] = xi
            xr, xi = step(xr, xi, s_scr[row:row + 1, 0:n_state], s_scr[row:row + 1, n_state:2 * n_state])
        xfs_ref[b:b + 1, 0:n_state] = xr
        xfs_ref[b:b + 1, n_state:2 * n_state] = xi

    dvec = d_ref[...]
    for r in range(n_rows // row_tile):
        rows = slice(r * row_tile, (r + 1) * row_tile)
        xst = xst_scr[rows, :].astype(BF16)
        for t in range(0, SSM_L, 2):
            k_ext = (t + 2) * ch
            y = (_dot(u_ref[rows, 0:k_ext], toep_scr[0:k_ext, t * ch:(t + 2) * ch])
                 + _dot_nt(xst, cmt_scr[(t + 1) * ch:(t + 3) * ch, :]))
            for tt in (t, t + 1):
                cols = slice(tt * ch, (tt + 1) * ch)
                yt = y[:, (tt - t) * ch:(tt - t + 1) * ch] + dvec * u_ref[rows, cols].astype(F32)
                g_ref[rows, cols] = _gelu_tanh(yt).astype(BF16)


def _ssm_core(u_blk, x0, prow, bmr, bmi, cmr, cmi, dvec, n_prompt, blocks_per_prompt, n_sample,
              blocks_per_sample, groups_per_tile):
    n_gb, nc, width = u_blk.shape
    ch = LANES
    sg, n_state = bmr.shape[1], bmr.shape[2]
    assert width == SSM_L * ch and nc == n_prompt * blocks_per_prompt + n_sample * blocks_per_sample
    row_tile = _tile(nc, 384)
    kern = functools.partial(_ssm_core_kernel, n_prompt=n_prompt, blocks_per_prompt=blocks_per_prompt,
                             n_sample=n_sample, blocks_per_sample=blocks_per_sample, row_tile=row_tile,
                             groups_per_tile=groups_per_tile)
    return pl.pallas_call(
        kern,
        grid=(n_gb,),
        in_specs=[pl.BlockSpec((None, nc, width), lambda g: (g, 0, 0)),
                  pl.BlockSpec((n_sample, 2 * n_state), lambda g: (0, g)),
                  pl.BlockSpec((None, 3, n_state), lambda g: (g, 0, 0)),
                  pl.BlockSpec((None, sg, n_state), lambda g: (g, 0, 0)),
                  pl.BlockSpec((None, sg, n_state), lambda g: (g, 0, 0)),
                  pl.BlockSpec((None, ch, LANES), lambda g: (g, 0, 0)),
                  pl.BlockSpec((None, ch, LANES), lambda g: (g, 0, 0)),
                  pl.BlockSpec((1, ch), lambda g: (0, g))],
        out_specs=[pl.BlockSpec((None, nc, width), lambda g: (g, 0, 0)),
                   pl.BlockSpec((n_prompt, 2 * n_state), lambda g: (0, g)),
                   pl.BlockSpec((n_sample, 2 * n_state), lambda g: (0, g))],
        out_shape=[jax.ShapeDtypeStruct((n_gb, nc, width), BF16),
                   jax.ShapeDtypeStruct((n_prompt, n_gb * 2 * n_state), F32),
                   jax.ShapeDtypeStruct((n_sample, n_gb * 2 * n_state), F32)],
        scratch_shapes=[pltpu.VMEM((width, 2 * n_state), BF16),
                        pltpu.VMEM((width + ch, 2 * n_state), BF16),
                        pltpu.VMEM((width, width), BF16),
                        pltpu.VMEM((nc, 2 * n_state), F32),
                        pltpu.VMEM((nc, 2 * n_state), F32)],
        compiler_params=_params("arbitrary"),
        name="ssm_core",
    )(u_blk, x0, prow, bmr, bmi, cmr, cmi, dvec)


def _glu_res_kernel(g_ref, x_ref, wa_ref, wb_ref, o_ref, slab_scr, lhs_scr):
    n_gb = g_ref.shape[0]
    tm = x_ref.shape[0]

    for gb in range(n_gb):
        for s in range(SSM_L):
            rows = pl.ds(gb * tm + s, tm // SSM_L, stride=SSM_L)
            slab_scr[rows, :] = g_ref[gb, :, s * LANES:(s + 1) * LANES].astype(F32)
    for gb in range(n_gb):
        lhs_scr[:, gb * LANES:(gb + 1) * LANES] = slab_scr[gb * tm:(gb + 1) * tm, :].astype(BF16)

    g = lhs_scr[...]
    a = _dot(g, wa_ref[...])
    b = _dot(g, wb_ref[...])
    o_ref[...] = x_ref[...] + a * jax.nn.sigmoid(b)


def _glu_res(x, g_blk, w_a_all, w_b_all, layer):
    t, d = x.shape
    n_gb, nc, _ = g_blk.shape
    tm = _tile(t, 512, SSM_L * SUBLANES_BF16)
    w_spec = pl.BlockSpec((None, d, d), lambda i: (layer, 0, 0), pipeline_mode=pl.Buffered(1))
    return pl.pallas_call(
        _glu_res_kernel,
        grid=(t // tm,),
        in_specs=[pl.BlockSpec((n_gb, tm // SSM_L, SSM_L * LANES), lambda i: (0, i, 0)),
                  pl.BlockSpec((tm, d), lambda i: (i, 0)),
                  w_spec, w_spec],
        out_specs=pl.BlockSpec((tm, d), lambda i: (i, 0)),
        out_shape=jax.ShapeDtypeStruct((t, d), F32),
        scratch_shapes=[pltpu.VMEM((n_gb * tm, LANES), F32),
                        pltpu.VMEM((tm, d), BF16)],
        compiler_params=_params("arbitrary"),
        name="glu_res",
    )(g_blk, x, w_a_all, w_b_all)


def _ssm_layouts(a_re, a_im, log_dt, b_re, b_im, c_re, c_im):
    g, p = a_re.shape
    sg = b_re.shape[2]
    gl = LANES // sg
    n_gb = g // gl
    dt = jnp.broadcast_to(log_dt[:, None], (g, p))
    prow = jnp.stack([a_re, a_im, dt], axis=0).reshape(3, n_gb, gl * p).transpose(1, 0, 2)

    def b_rows(b):
        return b.reshape(n_gb, gl, p, sg).transpose(0, 3, 1, 2).reshape(n_gb, sg, gl * p)

    def c_cols(c):
        return jnp.tile(c, (1, 1, LANES // p)).reshape(n_gb, gl * sg, LANES)

    return prow, b_rows(b_re), b_rows(b_im), c_cols(c_re), c_cols(c_im)


def _state_to_tiles(st_re, st_im, gl):
    b, g, p = st_re.shape
    n_gb = g // gl
    st = jnp.stack([st_re.reshape(b, n_gb, gl * p), st_im.reshape(b, n_gb, gl * p)], axis=2)
    return st.reshape(b, n_gb * 2 * gl * p)


def _tiles_to_state(x, g, p, gl):
    b = x.shape[0]
    n_gb = g // gl
    x = x.reshape(b, n_gb, 2, gl, p)
    return x[:, :, 0].reshape(b, g, p), x[:, :, 1].reshape(b, g, p)


def kernel(x_prompt, x_sample, cache_attn_k, cache_attn_v, state_ssm_re, state_ssm_im, norm_mix, norm_mlp, norm_final, attn_w_qkv, attn_w_o, attn_rel_bias, ssm_a_re, ssm_a_im, ssm_log_dt, ssm_b_re, ssm_b_im, ssm_c_re, ssm_c_im, ssm_d, ssm_w_glu_a, ssm_w_glu_b, mlp_w_up, mlp_w_down):
    bp, seq, d = x_prompt.shape
    bs, ts, _ = x_sample.shape
    depth = norm_mix.shape[0]
    n_heads, dh = cache_attn_k.shape[3], cache_attn_k.shape[4]
    n_groups, p_state = ssm_a_re.shape[1], ssm_a_re.shape[2]
    sg = d // n_groups
    gl = LANES // sg
    assert ts == CHUNK and seq % KV_TAIL == 0 and (bs * ts) % KV_TAIL == 0
    assert seq % SSM_L == 0 and ts % SSM_L == 0 and d % LANES == 0

    n_prompt_rows = bp * seq
    xs = (x_prompt.reshape(n_prompt_rows, d), x_sample.reshape(bs * ts, d))

    w_qkv = attn_w_qkv.astype(BF16)
    w_o = attn_w_o.astype(BF16)
    w_glu_a = ssm_w_glu_a.astype(BF16)
    w_glu_b = ssm_w_glu_b.astype(BF16)

    k_p, v_p, k_s, v_s = [], [], [], []
    sr_p, si_p, sr_s, si_s = [], [], [], []
    for i in range(depth):
        j = i // 2
        gain = norm_mix[i][None, :]
        if i % 2 == 0:
            qkv, kv_tail = _qkv(xs, gain, w_qkv, j, seq // KV_TAIL, bp, dh ** -0.5 * LOG2_E)
            bias = _bias_table(attn_rel_bias[j])
            o_p = _attn_prompt(qkv, bias, bp, seq, d, dh)
            o_s = _attn_sample(qkv, cache_attn_k, cache_attn_v, j, bias, n_prompt_rows, d, dh)
            x = _proj_res(xs, o_p, o_s, w_o, j)
            n_p, n_s = bp * KV_TAIL, bs * ts
            v0 = n_p + n_s
            k_p.append(kv_tail[:n_p].reshape(bp, KV_TAIL, n_heads, dh))
            k_s.append(kv_tail[n_p:v0].reshape(bs, ts, n_heads, dh))
            v_p.append(kv_tail[v0:v0 + n_p].reshape(bp, KV_TAIL, n_heads, dh))
            v_s.append(kv_tail[v0 + n_p:v0 + n_p + n_s].reshape(bs, ts, n_heads, dh))
        else:
            x = xs[0]
            layouts = _ssm_layouts(ssm_a_re[j], ssm_a_im[j], ssm_log_dt[j], ssm_b_re[j], ssm_b_im[j],
                                   ssm_c_re[j], ssm_c_im[j])
            u_blk = _norm_blocked(x, gain)
            x0 = _state_to_tiles(state_ssm_re[j], state_ssm_im[j], gl)
            g_blk, xf_p, xf_s = _ssm_core(u_blk, x0, *layouts, ssm_d[j][None, :], bp, seq // SSM_L, bs,
                                          ts // SSM_L, gl)
            x = _glu_res(x, g_blk, w_glu_a, w_glu_b, j)
            rp, ip = _tiles_to_state(xf_p, n_groups, p_state, gl)
            rs, is_ = _tiles_to_state(xf_s, n_groups, p_state, gl)
            sr_p.append(rp); si_p.append(ip); sr_s.append(rs); si_s.append(is_)
        x = _mlp(x, norm_mlp[i][None, :], mlp_w_up, mlp_w_down, i)
        xs = (x,)

    y_p, y_s = _final_norm(xs[0], norm_final[None, :], n_prompt_rows)
    return (y_p.reshape(bp, seq, d), y_s.reshape(bs, ts, d),
            jnp.stack(k_p), jnp.stack(v_p), jnp.stack(sr_p), jnp.stack(si_p),
            jnp.stack(k_s), jnp.stack(v_s), jnp.stack(sr_s), jnp.stack(si_s))
```
